```python
import jax, jax.numpy as jnp
from jax import lax
import numpy as np

D_MODEL = 1024
BATCH = 1
SEQ = 16384
DEPTH = 2

N_MIXERS = 2
N_DELTA_LAYERS = (DEPTH + 1) // 2
N_FOURIER_LAYERS = DEPTH // 2

N_K_HEADS = 8
N_V_HEADS = 16
HEAD_K_DIM = 128
HEAD_V_DIM = 128
KEY_DIM = N_K_HEADS * HEAD_K_DIM
VALUE_DIM = N_V_HEADS * HEAD_V_DIM
CONV_DIM = 2 * KEY_DIM + VALUE_DIM
CONV_WIDTH = 5
CHUNK = 64
N_DIRS = 2
IN_PROJ_DIM = CONV_DIM + VALUE_DIM + 2 * N_DIRS * N_V_HEADS

N_FOURIER_GROUPS = 4
FOURIER_GROUP_DIM = D_MODEL // N_FOURIER_GROUPS

D_FF = -(-8 * D_MODEL // (3 * 256)) * 256

RMS_EPS = 1e-6
L2_EPS = 1e-6

kernel_name = "hybrid_deltanet_fnet_encoder"


def rmsnorm(x, w):
    xf = x.astype(jnp.float32)
    y = xf * lax.rsqrt(jnp.mean(xf * xf, axis=-1, keepdims=True) + RMS_EPS)
    return (y * w.astype(jnp.float32)).astype(x.dtype)


def l2norm(x):
    xf = x.astype(jnp.float32)
    return xf * lax.rsqrt(jnp.sum(xf * xf, axis=-1, keepdims=True) + L2_EPS)


def centred_depthwise_conv(x, w):
    width = w.shape[0]
    return lax.conv_general_dilated(
        x, w[:, None, :].astype(x.dtype), window_strides=(1,),
        padding=[((width - 1) // 2, width // 2)],
        dimension_numbers=("NWC", "WIO", "NWC"),
        feature_group_count=x.shape[-1])


def chunk_gated_delta_rule(q, k, v, g, beta):
    B, H, L, dk = k.shape
    dv = v.shape[-1]
    n = L // CHUNK
    q = q * (dk ** -0.5)
    q, k, v = (t.reshape(B, H, n, CHUNK, t.shape[-1]) for t in (q, k, v))
    beta = beta.reshape(B, H, n, CHUNK, 1)
    g = jnp.cumsum(g.reshape(B, H, n, CHUNK), axis=-1)
    k_beta = k * beta
    v_beta = v * beta
    causal = jnp.tril(jnp.ones((CHUNK, CHUNK), dtype=bool))
    strict = jnp.tril(jnp.ones((CHUNK, CHUNK), dtype=bool), k=-1)
    diff = g[..., :, None] - g[..., None, :]
    decay = jnp.where(causal, jnp.exp(jnp.where(causal, diff, 0.0)), 0.0)
    lower = jnp.where(strict, jnp.einsum("bhncd,bhnsd->bhncs", k_beta, k) * decay, 0.0)
    a_mat = lower + jnp.eye(CHUNK, dtype=lower.dtype)
    rhs = jnp.concatenate([v_beta, k_beta * jnp.exp(g)[..., None]], axis=-1)
    sol = lax.linalg.triangular_solve(a_mat, rhs, left_side=True, lower=True, unit_diagonal=True)
    u, w = sol[..., :dv], sol[..., dv:]
    qk = jnp.where(causal, jnp.einsum("bhncd,bhnsd->bhncs", q, k) * decay, 0.0)
    q_dec = q * jnp.exp(g)[..., None]
    k_dec = k * jnp.exp(g[..., -1:] - g)[..., None]
    g_last = jnp.exp(g[..., -1])

    def step(state, inp):
        qk_c, qd_c, kd_c, u_c, w_c, gl_c = inp
        v_new = u_c - jnp.einsum("bhcd,bhde->bhce", w_c, state)
        o = jnp.einsum("bhcd,bhde->bhce", qd_c, state) + jnp.einsum("bhcs,bhse->bhce", qk_c, v_new)
        state = state * gl_c[..., None, None] + jnp.einsum("bhcd,bhce->bhde", kd_c, v_new)
        return state, o

    xs = tuple(jnp.moveaxis(t, 2, 0) for t in (qk, q_dec, k_dec, u, w, g_last))
    _, o = lax.scan(step, jnp.zeros((B, H, dk, dv), jnp.float32), xs)
    return jnp.moveaxis(o, 0, 2).reshape(B, H, L, dv)


def gated_deltanet_mixer(h, w_in, conv_w, a_log, dt_bias, out_norm_w, w_out):
    B, L, _ = h.shape
    f32 = jnp.float32
    proj = h @ w_in
    qkv, z, b, a = jnp.split(
        proj, [CONV_DIM, CONV_DIM + VALUE_DIM, CONV_DIM + VALUE_DIM + N_DIRS * N_V_HEADS], axis=-1)
    qkv = jax.nn.silu(centred_depthwise_conv(qkv, conv_w))
    q, k, v = jnp.split(qkv, [KEY_DIM, 2 * KEY_DIM], axis=-1)
    rep = N_V_HEADS // N_K_HEADS
    q = jnp.repeat(l2norm(q.reshape(B, L, N_K_HEADS, HEAD_K_DIM)), rep, axis=2)
    k = jnp.repeat(l2norm(k.reshape(B, L, N_K_HEADS, HEAD_K_DIM)), rep, axis=2)
    v = v.reshape(B, L, N_V_HEADS, HEAD_V_DIM).astype(f32)
    q, k, v = (jnp.swapaxes(t, 1, 2) for t in (q, k, v))
    b = b.reshape(B, L, N_DIRS, N_V_HEADS).astype(f32)
    a = a.reshape(B, L, N_DIRS, N_V_HEADS).astype(f32)
    beta = jnp.transpose(jax.nn.sigmoid(b), (2, 0, 3, 1))
    g = -jnp.exp(a_log.astype(f32)) * jax.nn.softplus(a + dt_bias.astype(f32))
    g = jnp.transpose(g, (2, 0, 3, 1))
    o_fwd = chunk_gated_delta_rule(q, k, v, g[0], beta[0])
    flip = lambda t: jnp.flip(t, axis=2)
    o_bwd = flip(chunk_gated_delta_rule(flip(q), flip(k), flip(v), flip(g[1]), flip(beta[1])))
    o = jnp.swapaxes(o_fwd + o_bwd, 1, 2)
    z = z.reshape(B, L, N_V_HEADS, HEAD_V_DIM).astype(f32)
    o = rmsnorm(o, out_norm_w) * jax.nn.silu(z)
    return o.reshape(B, L, VALUE_DIM).astype(h.dtype) @ w_out


def fourier_mixer(h, w_out, b_out):
    B, L, D = h.shape
    hg = h.astype(jnp.float32).reshape(B, L, N_FOURIER_GROUPS, FOURIER_GROUP_DIM)
    mixed = jnp.fft.fftn(hg, axes=(1, 3), norm="ortho").real
    return mixed.reshape(B, L, D).astype(h.dtype) @ w_out + b_out


def swiglu_ffn(h, w_gate_up, w_down):
    gate, up = jnp.split(h @ w_gate_up, 2, axis=-1)
    return (jax.nn.silu(gate) * up) @ w_down


def setup_inputs(seed: int = 0) -> dict:
    key = jax.random.key(seed)
    ks = jax.random.split(key, 16)
    f32 = jnp.float32
    nrm = lambda k, shape, fan_in: jax.random.normal(k, shape, f32) * (fan_in ** -0.5)
    gain = lambda k, shape: 1.0 + 0.01 * jax.random.normal(k, shape, f32)
    x = jax.random.normal(ks[0], (BATCH, SEQ, D_MODEL), f32)
    mix_norm_w = gain(ks[1], (DEPTH, D_MODEL))
    ffn_norm_w = gain(ks[2], (DEPTH, D_MODEL))
    dn_w_in = nrm(ks[3], (N_DELTA_LAYERS, D_MODEL, IN_PROJ_DIM), D_MODEL)
    dn_conv_w = nrm(ks[4], (N_DELTA_LAYERS, CONV_WIDTH, CONV_DIM), CONV_WIDTH)
    dn_a_log = jnp.log(jax.random.uniform(ks[5], (N_DELTA_LAYERS, N_DIRS, N_V_HEADS), f32, 1.0, 16.0))
    dt = jnp.exp(jax.random.uniform(ks[6], (N_DELTA_LAYERS, N_DIRS, N_V_HEADS), f32,
                                    np.log(1e-3).astype(np.float32), np.log(1e-1).astype(np.float32)))
    dn_dt_bias = dt + jnp.log(-jnp.expm1(-dt))
    dn_out_norm_w = gain(ks[7], (N_DELTA_LAYERS, HEAD_V_DIM))
    dn_w_out = nrm(ks[8], (N_DELTA_LAYERS, VALUE_DIM, D_MODEL), VALUE_DIM)
    fn_w_out = nrm(ks[9], (N_FOURIER_LAYERS, D_MODEL, D_MODEL), D_MODEL)
    fn_b_out = 0.01 * jax.random.normal(ks[10], (N_FOURIER_LAYERS, D_MODEL), f32)
    ffn_w_gate_up = nrm(ks[11], (DEPTH, D_MODEL, 2 * D_FF), D_MODEL)
    ffn_w_down = nrm(ks[12], (DEPTH, D_FF, D_MODEL), D_FF)
    final_norm_w = gain(ks[13], (D_MODEL,))
    return {"x": x, "mix_norm_w": mix_norm_w, "ffn_norm_w": ffn_norm_w,
            "dn_w_in": dn_w_in, "dn_conv_w": dn_conv_w, "dn_a_log": dn_a_log,
            "dn_dt_bias": dn_dt_bias, "dn_out_norm_w": dn_out_norm_w, "dn_w_out": dn_w_out,
            "fn_w_out": fn_w_out, "fn_b_out": fn_b_out,
            "ffn_w_gate_up": ffn_w_gate_up, "ffn_w_down": ffn_w_down,
            "final_norm_w": final_norm_w}


def reference(x, mix_norm_w, ffn_norm_w, dn_w_in, dn_conv_w, dn_a_log, dn_dt_bias,
              dn_out_norm_w, dn_w_out, fn_w_out, fn_b_out, ffn_w_gate_up, ffn_w_down,
              final_norm_w):
    for i in range(DEPTH):
        h = rmsnorm(x, mix_norm_w[i])
        j = i // N_MIXERS
        if i % N_MIXERS == 0:
            x = x + gated_deltanet_mixer(h, dn_w_in[j], dn_conv_w[j], dn_a_log[j], dn_dt_bias[j],
                                         dn_out_norm_w[j], dn_w_out[j])
        else:
            x = x + fourier_mixer(h, fn_w_out[j], fn_b_out[j])
        x = x + swiglu_ffn(rmsnorm(x, ffn_norm_w[i]), ffn_w_gate_up[i], ffn_w_down[i])
    return rmsnorm(x, final_norm_w)
```

```python
import functools
import math

import jax
import jax.numpy as jnp
from jax import lax
from jax.experimental import pallas as pl
from jax.experimental.pallas import tpu as pltpu

F32 = jnp.float32
BF16 = jnp.bfloat16

LANES_V7X = 128
SUBLANES_V7X = 8
VMEM_BYTES_V7X = 64 * 1024 * 1024
VMEM_LIMIT_CAP = VMEM_BYTES_V7X - 8 * 1024 * 1024

D_MODEL = 1024
N_K_HEADS = 8
N_V_HEADS = 16
HEAD_DIM = 128
KEY_DIM = N_K_HEADS * HEAD_DIM
VALUE_DIM = N_V_HEADS * HEAD_DIM
CONV_DIM = 2 * KEY_DIM + VALUE_DIM
CONV_WIDTH = 5
CONV_HALO = SUBLANES_V7X
CHUNK = 64
N_DIRS = 2
KH_COLS = 2 * HEAD_DIM + 2 * HEAD_DIM
N_GATE_ROWS = 8
N_FOURIER_GROUPS = 4
GROUP_DIM = D_MODEL // N_FOURIER_GROUPS
D_FF = 2816
RMS_EPS = 1e-6
L2_EPS = 1e-6

DN_TILE = 512
SCAN_GROUP = 4


def _params(semantics, *block_bytes):
    need = 2 * sum(block_bytes) + 24 * 1024 * 1024
    return pltpu.CompilerParams(dimension_semantics=semantics,
                                vmem_limit_bytes=int(min(need, VMEM_LIMIT_CAP)))


def _rms(x, w):
    return x * lax.rsqrt(jnp.mean(x * x, axis=-1, keepdims=True) + RMS_EPS) * w


def _silu(x):
    return x * jax.nn.sigmoid(x)


def _softplus(x):
    return jnp.maximum(x, 0.0) + jnp.log1p(jnp.exp(-jnp.abs(x)))


def _dot(a, b):
    return jnp.dot(a, b, preferred_element_type=F32)


def _dot_nt(a, b):
    return lax.dot_general(a, b, (((1,), (1,)), ((), ())), preferred_element_type=F32)


def _inproj_body(x_ref, nw_ref, wqkv_ref, wz_ref, wg_ref, qkv_ref, z_ref, g_ref, h_ref):
    @pl.when(pl.program_id(1) == 0)
    def _():
        h = _rms(x_ref[...], nw_ref[...]).astype(BF16)
        h_ref[...] = h
        g_ref[...] = _dot(h, wg_ref[...])

    h = h_ref[...]
    qkv_ref[...] = _dot(h, wqkv_ref[...])
    z_ref[...] = _dot(h, wz_ref[...]).astype(BF16)


def _in_proj(x, nw, wqkv, wz, wg, tm=1024, nj=4):
    L, D = x.shape
    tq, tz = wqkv.shape[1] // nj, wz.shape[1] // nj
    return pl.pallas_call(
        _inproj_body,
        grid=(L // tm, nj),
        in_specs=[pl.BlockSpec((tm, D), lambda i, j: (i, 0)),
                  pl.BlockSpec((1, D), lambda i, j: (0, 0)),
                  pl.BlockSpec((D, tq), lambda i, j: (0, j)),
                  pl.BlockSpec((D, tz), lambda i, j: (0, j)),
                  pl.BlockSpec((D, LANES_V7X), lambda i, j: (0, 0))],
        out_specs=[pl.BlockSpec((tm, tq), lambda i, j: (i, j)),
                   pl.BlockSpec((tm, tz), lambda i, j: (i, j)),
                   pl.BlockSpec((tm, LANES_V7X), lambda i, j: (i, 0))],
        out_shape=[jax.ShapeDtypeStruct((L, wqkv.shape[1]), F32),
                   jax.ShapeDtypeStruct((L, wz.shape[1]), BF16),
                   jax.ShapeDtypeStruct((L, LANES_V7X), F32)],
        scratch_shapes=[pltpu.VMEM((tm, D), BF16)],
        compiler_params=_params(("parallel", "arbitrary"), tm * D * 4, D * tq * 2, D * tz * 2,
                                tm * tq * 4, tm * tz * 2, tm * LANES_V7X * 4, tm * D * 2),
        name="dn_in_proj",
    )(x, nw, wqkv, wz, wg)


def _block_diag4(x):
    blk = lax.broadcasted_iota(jnp.int32, x.shape, 1) // CHUNK
    zero = jnp.zeros_like(x)
    return jnp.concatenate([jnp.where(blk == r, x, zero) for r in range(4)], axis=0)


def _block_diag2(x):
    left = lax.broadcasted_iota(jnp.int32, x.shape, 1) < HEAD_DIM
    zero = jnp.zeros_like(x)
    return jnp.concatenate([jnp.where(left, x, zero), jnp.where(left, zero, x)], axis=0)


def _dn_prep_body(xm_ref, xp_ref, xn_ref, cw_ref, gr_ref, alog_ref, dtb_ref,
                  w_ref, u_ref, qd_ref, qk_ref, kdt_ref, gl_ref, xs_ref):
    i = pl.program_id(1)
    nt = pl.num_programs(1)
    T = xm_ref.shape[0]
    nch = T // CHUNK

    xs_ref[0:CONV_HALO, :] = jnp.where(i > 0, xp_ref[...], 0.0)
    xs_ref[CONV_HALO:CONV_HALO + T, :] = xm_ref[...]
    xs_ref[CONV_HALO + T:2 * CONV_HALO + T, :] = jnp.where(i < nt - 1, xn_ref[...], 0.0)
    first = CONV_HALO - (CONV_WIDTH - 1) // 2
    y = xs_ref[pl.ds(first, T), :] * cw_ref[0:1, :]
    for tap in range(1, CONV_WIDTH):
        y = y + xs_ref[pl.ds(first + tap, T), :] * cw_ref[tap:tap + 1, :]
    y = _silu(y)
    q = y[:, 0:HEAD_DIM]
    k = y[:, HEAD_DIM:2 * HEAD_DIM]
    v = y[:, 2 * HEAD_DIM:]
    q = q * lax.rsqrt(jnp.sum(q * q, axis=-1, keepdims=True) + L2_EPS) * (HEAD_DIM ** -0.5)
    k = k * lax.rsqrt(jnp.sum(k * k, axis=-1, keepdims=True) + L2_EPS)

    rows = nch * N_GATE_ROWS
    gr = gr_ref[...].reshape(rows, LANES_V7X)
    alog = jnp.concatenate([alog_ref[...]] * nch, axis=0)
    dtb = jnp.concatenate([dtb_ref[...]] * nch, axis=0)
    g = -jnp.exp(alog) * _softplus(gr + dtb)
    beta = jax.nn.sigmoid(gr)
    lane = lax.broadcasted_iota(jnp.int32, (rows, LANES_V7X), 1)
    l64 = lane % CHUNK
    unit = lax.broadcasted_iota(jnp.int32, (rows, LANES_V7X), 0) % N_GATE_ROWS
    pre, suf = g, g
    step = 1
    while step < CHUNK:
        pre = pre + jnp.where(l64 >= step, pltpu.roll(pre, step, 1), 0.0)
        suf = suf + jnp.where(l64 < CHUNK - step, pltpu.roll(suf, LANES_V7X - step, 1), 0.0)
        step *= 2
    cum = jnp.where(unit < 2, pre, suf)
    tot = pre + suf - g
    ecum = jnp.exp(cum)
    edec = jnp.exp(tot - cum)
    gl = jnp.exp(tot)

    cb = jnp.where(unit < 4, cum, beta)
    assert 2 * rows == LANES_V7X
    cbt = jnp.transpose(jnp.concatenate([cb, cb], axis=0))
    ecbt = jnp.exp(cbt)

    def col(mat, c, r):
        idx = c * N_GATE_ROWS + r
        return jnp.broadcast_to(mat[0:CHUNK, idx:idx + 1], (CHUNK, LANES_V7X))

    def row(mat, c, r):
        idx = c * N_GATE_ROWS + r
        return mat[idx:idx + 1, :]

    ri = lax.broadcasted_iota(jnp.int32, (CHUNK, LANES_V7X), 0)
    li = lax.broadcasted_iota(jnp.int32, (CHUNK, LANES_V7X), 1)
    left = li < CHUNK
    li64 = li % CHUNK
    eye4 = jnp.concatenate([(ri == li64).astype(F32)] * 2, axis=1)

    for c in range(nch):
        sl = slice(c * CHUNK, (c + 1) * CHUNK)
        qc, kc = q[sl], k[sl]
        kcb = kc.astype(BF16)
        kk2 = jnp.concatenate([kcb, kcb], axis=0)
        qkk = _dot_nt(jnp.concatenate([qc.astype(BF16), kcb], axis=0), kk2)
        qk2, kk = qkk[0:CHUNK], qkk[CHUNK:]
        kt2 = jnp.transpose(jnp.concatenate([kc, kc], axis=0))

        n_pairs, beta_rows, ecum_rows = [], [], []
        for p in range(N_DIRS):
            r0, r1 = 2 * p, 2 * p + 1
            cum_row = jnp.where(left[0:1], row(cum, c, r0), row(cum, c, r1))
            cum_col = jnp.where(left, col(cbt, c, r0), col(cbt, c, r1))
            beta_col = jnp.where(left, col(cbt, c, 4 + r0), col(cbt, c, 4 + r1))
            incl = (ri >= li64) if p == 0 else (ri <= li64)
            strict = (ri > li64) if p == 0 else (ri < li64)
            dm = jnp.where(incl, jnp.exp(jnp.where(incl, cum_col - cum_row, 0.0)), 0.0)
            n_pairs.append(jnp.where(strict, kk * dm * beta_col, 0.0))
            qk_ref[p, sl, :] = (qk2 * dm).astype(BF16)
            qd_ref[p, sl, :] = jnp.concatenate(
                [qc * col(ecbt, c, r0), qc * col(ecbt, c, r1)], axis=1).astype(BF16)
            beta_rows.append(jnp.where(left[0:1], row(beta, c, 4 + r0), row(beta, c, 4 + r1)))
            ecum_rows.append(jnp.where(left[0:1], row(ecum, c, r0), row(ecum, c, r1)))
            edec_row = jnp.where(left[0:1], row(edec, c, r0), row(edec, c, r1))
            kdt_ref[p, c * 2 * CHUNK:(c + 1) * 2 * CHUNK, :] = (kt2 * edec_row).astype(BF16)
            gl_ref[p, c:c + 1, :] = jnp.concatenate([row(gl, c, r0), row(gl, c, r1)], axis=1)

        pm = -jnp.concatenate(n_pairs, axis=1)
        inv = eye4 + pm
        pb = pm.astype(BF16)
        pw = _dot(pb, _block_diag4(pb))
        for _ in range(4):
            pb = pw.astype(BF16)
            res = _dot(jnp.concatenate([inv.astype(BF16), pb], axis=0), _block_diag4(pb))
            inv = inv + res[0:CHUNK]
            pw = res[CHUNK:]
        inv = inv + _dot(inv.astype(BF16), _block_diag4(pw.astype(BF16)))

        t1 = inv * jnp.concatenate(beta_rows, axis=1)
        t2 = t1 * jnp.concatenate(ecum_rows, axis=1)
        t1s = jnp.concatenate([t1[:, 0:LANES_V7X], t1[:, LANES_V7X:]], axis=0).astype(BF16)
        t2s = jnp.concatenate([t2[:, 0:LANES_V7X], t2[:, LANES_V7X:]], axis=0).astype(BF16)
        wv = _dot(t2s, _block_diag2(jnp.concatenate([kcb, kcb], axis=1)))
        uv = _dot(t1s, _block_diag2(v[sl].astype(BF16)))
        for p in range(N_DIRS):
            w_ref[p, sl, :] = wv[p * CHUNK:(p + 1) * CHUNK].astype(BF16)
            u_ref[p, sl, :] = uv[p * CHUNK:(p + 1) * CHUNK].astype(BF16)


def _dn_prep(qkv, cw, gates_row, alog_b, dtb_b, T=DN_TILE):
    L = qkv.shape[0]
    nt, nch, nc = L // T, T // CHUNK, L // CHUNK
    hb = T // CONV_HALO
    last_halo = L // CONV_HALO - 1
    pair = 2 * HEAD_DIM
    out_shapes = [jax.ShapeDtypeStruct((N_DIRS, N_K_HEADS, L, pair), BF16),
                  jax.ShapeDtypeStruct((N_DIRS, N_K_HEADS, L, pair), BF16),
                  jax.ShapeDtypeStruct((N_DIRS, N_K_HEADS, L, pair), BF16),
                  jax.ShapeDtypeStruct((N_DIRS, N_K_HEADS, L, HEAD_DIM), BF16),
                  jax.ShapeDtypeStruct((N_DIRS, N_K_HEADS, nc * 2 * CHUNK, HEAD_DIM), BF16),
                  jax.ShapeDtypeStruct((N_DIRS, N_K_HEADS, nc, pair), F32)]
    big = lambda h, i: (0, h, i, 0)
    return pl.pallas_call(
        _dn_prep_body,
        grid=(N_K_HEADS, nt),
        in_specs=[pl.BlockSpec((T, KH_COLS), lambda h, i: (i, h)),
                  pl.BlockSpec((CONV_HALO, KH_COLS), lambda h, i: (jnp.maximum(i * hb - 1, 0), h)),
                  pl.BlockSpec((CONV_HALO, KH_COLS), lambda h, i: (jnp.minimum((i + 1) * hb, last_halo), h)),
                  pl.BlockSpec((SUBLANES_V7X, KH_COLS), lambda h, i: (0, h)),
                  pl.BlockSpec((None, nch, N_GATE_ROWS, LANES_V7X), lambda h, i: (h, i, 0, 0)),
                  pl.BlockSpec((None, N_GATE_ROWS, LANES_V7X), lambda h, i: (h, 0, 0)),
                  pl.BlockSpec((None, N_GATE_ROWS, LANES_V7X), lambda h, i: (h, 0, 0))],
        out_specs=[pl.BlockSpec((N_DIRS, None, T, pair), big),
                   pl.BlockSpec((N_DIRS, None, T, pair), big),
                   pl.BlockSpec((N_DIRS, None, T, pair), big),
                   pl.BlockSpec((N_DIRS, None, T, HEAD_DIM), big),
                   pl.BlockSpec((N_DIRS, None, nch * 2 * CHUNK, HEAD_DIM), big),
                   pl.BlockSpec((N_DIRS, None, nch, pair), big)],
        out_shape=out_shapes,
        scratch_shapes=[pltpu.VMEM((T + 2 * CONV_HALO, KH_COLS), F32)],
        compiler_params=_params(("parallel", "parallel"), T * KH_COLS * 4, 3 * N_DIRS * T * pair * 2,
                                N_DIRS * T * HEAD_DIM * 2 * 2),
        name="dn_chunk_prep",
    )(qkv, qkv, qkv, cw, gates_row, alog_b, dtb_b)


def _dn_scan_body(w_ref, u_ref, qd_ref, qk_ref, kdt_ref, gl_ref, o_ref, s_ref, *, n_groups):
    grp = pl.program_id(0)
    bwd = grp >= n_groups // N_DIRS
    G, T = w_ref.shape[0], w_ref.shape[1]
    nch = T // CHUNK

    @pl.when(pl.program_id(1) == 0)
    def _():
        s_ref[...] = jnp.zeros_like(s_ref)

    states = [s_ref[g] for g in range(G)]
    for c in range(nch):
        ce = jnp.where(bwd, nch - 1 - c, c)
        r0 = pl.multiple_of(ce * CHUNK, CHUNK)
        r1 = pl.multiple_of(ce * 2 * CHUNK, 2 * CHUNK)
        for g in range(G):
            s = states[g]
            rows = pl.ds(r0, CHUNK)
            res_a = _dot(jnp.concatenate([w_ref[g, rows, :], qd_ref[g, rows, :]], axis=0),
                         _block_diag2(s.astype(BF16)))
            v_new = u_ref[g, rows, :].astype(F32) - res_a[0:CHUNK]
            res_b = _dot(jnp.concatenate([qk_ref[g, rows, :], kdt_ref[g, pl.ds(r1, 2 * CHUNK), :]], axis=0),
                         _block_diag2(v_new.astype(BF16)))
            o_ref[rows, g * 2 * HEAD_DIM:(g + 1) * 2 * HEAD_DIM] = res_a[CHUNK:] + res_b[0:CHUNK]
            states[g] = s * gl_ref[g, pl.ds(ce, 1), :] + res_b[CHUNK:]
    for g in range(G):
        s_ref[g] = states[g]


def _dn_scan(w, u, qd, qk, kdt, gl, T=DN_TILE, G=SCAN_GROUP):
    nu, L, pair = w.shape
    nt, nch = L // T, T // CHUNK
    n_groups = nu // G
    per_dir = n_groups // N_DIRS

    def tile(gi, i):
        return jnp.where(gi >= per_dir, nt - 1 - i, i)

    big = lambda gi, i: (gi, tile(gi, i), 0)
    return pl.pallas_call(
        functools.partial(_dn_scan_body, n_groups=n_groups),
        grid=(n_groups, nt),
        in_specs=[pl.BlockSpec((G, T, pair), big),
                  pl.BlockSpec((G, T, pair), big),
                  pl.BlockSpec((G, T, pair), big),
                  pl.BlockSpec((G, T, HEAD_DIM), big),
                  pl.BlockSpec((G, nch * 2 * CHUNK, HEAD_DIM), big),
                  pl.BlockSpec((G, nch, pair), big)],
        out_specs=pl.BlockSpec((None, T, G * pair), lambda gi, i: (gi // per_dir, tile(gi, i), gi % per_dir)),
        out_shape=jax.ShapeDtypeStruct((N_DIRS, L, VALUE_DIM), F32),
        scratch_shapes=[pltpu.VMEM((G, HEAD_DIM, pair), F32)],
        compiler_params=_params(("parallel", "arbitrary"), 3 * G * T * pair * 2, 2 * G * T * HEAD_DIM * 2,
                                T * G * pair * 4),
        name="dn_state_scan",
    )(w, u, qd, qk, kdt, gl)


def _dn_out_body(of_ref, ob_ref, z_ref, nw_ref, wout_ref, res_ref, out_ref):
    nw = nw_ref[...]
    pieces = []
    for h in range(N_V_HEADS):
        sl = slice(h * HEAD_DIM, (h + 1) * HEAD_DIM)
        o = of_ref[:, sl] + ob_ref[:, sl]
        y = _rms(o, nw) * _silu(z_ref[:, sl].astype(F32))
        pieces.append(y.astype(BF16))
    out_ref[...] = res_ref[...] + _dot(jnp.concatenate(pieces, axis=1), wout_ref[...])


def _dn_out(o, z, nw, wout, res, tm=512):
    L, D = res.shape
    V = z.shape[1]
    return pl.pallas_call(
        _dn_out_body,
        grid=(L // tm,),
        in_specs=[pl.BlockSpec((None, tm, V), lambda i: (0, i, 0)),
                  pl.BlockSpec((None, tm, V), lambda i: (1, i, 0)),
                  pl.BlockSpec((tm, V), lambda i: (i, 0)),
                  pl.BlockSpec((1, HEAD_DIM), lambda i: (0, 0)),
                  pl.BlockSpec((V, D), lambda i: (0, 0)),
                  pl.BlockSpec((tm, D), lambda i: (i, 0))],
        out_specs=pl.BlockSpec((tm, D), lambda i: (i, 0)),
        out_shape=jax.ShapeDtypeStruct((L, D), F32),
        compiler_params=_params(("parallel",), 2 * tm * V * 4, tm * V * 2, V * D * 2, 2 * tm * D * 4),
        name="dn_out_proj",
    )(o, o, z, nw, wout, res)


def _ffn_up_body(x_ref, nw_ref, wg_ref, wu_ref, a_ref, h_ref):
    @pl.when(pl.program_id(1) == 0)
    def _():
        h_ref[...] = _rms(x_ref[...], nw_ref[...]).astype(BF16)

    h = h_ref[...]
    a_ref[...] = (_silu(_dot(h, wg_ref[...])) * _dot(h, wu_ref[...])).astype(BF16)


def _ffn_up(x, nw, wgu, tm=1024, nj=2):
    L, D = x.shape
    dff = wgu.shape[1] // 2
    tn = dff // nj
    return pl.pallas_call(
        _ffn_up_body,
        grid=(L // tm, nj),
        in_specs=[pl.BlockSpec((tm, D), lambda i, j: (i, 0)),
                  pl.BlockSpec((1, D), lambda i, j: (0, 0)),
                  pl.BlockSpec((D, tn), lambda i, j: (0, j)),
                  pl.BlockSpec((D, tn), lambda i, j: (0, nj + j))],
        out_specs=pl.BlockSpec((tm, tn), lambda i, j: (i, j)),
        out_shape=jax.ShapeDtypeStruct((L, dff), BF16),
        scratch_shapes=[pltpu.VMEM((tm, D), BF16)],
        compiler_params=_params(("parallel", "arbitrary"), tm * D * 4, 2 * D * tn * 2, tm * tn * 2,
                                tm * tn * 4),
        name="ffn_gate_up",
    )(x, nw, wgu, wgu)


def _ffn_down_body(a_ref, w_ref, res_ref, nw_ref, o_ref, *, final_norm):
    y = res_ref[...] + _dot(a_ref[...], w_ref[...])
    o_ref[...] = _rms(y, nw_ref[...]) if final_norm else y


def _ffn_down(a, w, res, nw, final_norm, tm=512):
    L, D = res.shape
    dff = a.shape[1]
    return pl.pallas_call(
        functools.partial(_ffn_down_body, final_norm=final_norm),
        grid=(L // tm,),
        in_specs=[pl.BlockSpec((tm, dff), lambda i: (i, 0)),
                  pl.BlockSpec((dff, D), lambda i: (0, 0)),
                  pl.BlockSpec((tm, D), lambda i: (i, 0)),
                  pl.BlockSpec((1, D), lambda i: (0, 0))],
        out_specs=pl.BlockSpec((tm, D), lambda i: (i, 0)),
        out_shape=jax.ShapeDtypeStruct((L, D), F32),
        compiler_params=_params(("parallel",), tm * dff * 2, dff * D * 2, 2 * tm * D * 4),
        name="ffn_down",
    )(a, w, res, nw)


def _fourier1_body(x_ref, nw_ref, tab_ref, y_ref):
    nb = tab_ref.shape[0]
    n1 = x_ref.shape[0]
    D = nw_ref.shape[1]
    for j in range(nb):
        h = _rms(x_ref[:, j * D:(j + 1) * D], nw_ref[...]).astype(BF16)
        y = _dot(tab_ref[j], h)
        y_ref[0, j] = y[0:n1].astype(BF16)
        y_ref[1, j] = y[n1:].astype(BF16)


def _fourier2_body(f_ref, y_ref, z_ref):
    z_ref[...] = _dot(f_ref[...], y_ref[...]).astype(BF16)


def _fourier_out_body(pr_ref, pi_ref, cc_ref, sc_ref, w_ref, b_ref, res_ref, o_ref):
    pieces = []
    for g in range(N_FOURIER_GROUPS):
        sl = slice(g * GROUP_DIM, (g + 1) * GROUP_DIM)
        m = _dot(pr_ref[:, sl], cc_ref[...]) + _dot(pi_ref[:, sl], sc_ref[...])
        pieces.append(m.astype(BF16))
    o_ref[...] = res_ref[...] + (_dot(jnp.concatenate(pieces, axis=1), w_ref[...]) + b_ref[...])


def _dft_tables(L, n1, n2):
    two_pi = 2.0 * math.pi
    i2 = lax.broadcasted_iota(jnp.int32, (n2, n1, n1), 0)
    k1 = lax.broadcasted_iota(jnp.int32, (n2, n1, n1), 1)
    i1 = lax.broadcasted_iota(jnp.int32, (n2, n1, n1), 2)
    ang = ((k1 * (i2 + n2 * i1)) % L).astype(F32) * (two_pi / L)
    s1 = 1.0 / math.sqrt(n1)
    tab1 = jnp.concatenate([jnp.cos(ang) * s1, -jnp.sin(ang) * s1], axis=1).astype(BF16)
    a = lax.broadcasted_iota(jnp.int32, (n2, n2), 0)
    b = lax.broadcasted_iota(jnp.int32, (n2, n2), 1)
    ang2 = ((a * b) % n2).astype(F32) * (two_pi / n2)
    s2 = 1.0 / math.sqrt(n2)
    c2, sn2 = jnp.cos(ang2) * s2, jnp.sin(ang2) * s2
    tab2 = jnp.concatenate([jnp.concatenate([c2, sn2], axis=1),
                            jnp.concatenate([-sn2, c2], axis=1)], axis=0).astype(BF16)
    a = lax.broadcasted_iota(jnp.int32, (GROUP_DIM, GROUP_DIM), 0)
    b = lax.broadcasted_iota(jnp.int32, (GROUP_DIM, GROUP_DIM), 1)
    angc = ((a * b) % GROUP_DIM).astype(F32) * (two_pi / GROUP_DIM)
    sc = 1.0 / math.sqrt(GROUP_DIM)
    return tab1, tab2, (jnp.cos(angc) * sc).astype(BF16), (jnp.sin(angc) * sc).astype(BF16)


def _fourier_mixer(x, nw, wout, bout, nb=4, tm=512):
    L, D = x.shape
    n1 = n2 = math.isqrt(L)
    assert n1 * n2 == L
    tab1, tab2, cc, sc = _dft_tables(L, n1, n2)
    y = pl.pallas_call(
        _fourier1_body,
        grid=(n2 // nb,),
        in_specs=[pl.BlockSpec((n1, nb * D), lambda i: (0, i)),
                  pl.BlockSpec((1, D), lambda i: (0, 0)),
                  pl.BlockSpec((nb, 2 * n1, n1), lambda i: (i, 0, 0))],
        out_specs=pl.BlockSpec((2, nb, n1, D), lambda i: (0, i, 0, 0)),
        out_shape=jax.ShapeDtypeStruct((2, n2, n1, D), BF16),
        compiler_params=_params(("parallel",), n1 * nb * D * 4, nb * 2 * n1 * n1 * 2, 2 * nb * n1 * D * 2),
        name="fourier_stage1",
    )(x.reshape(n1, n2 * D), nw, tab1)
    tn = nb * D
    z = pl.pallas_call(
        _fourier2_body,
        grid=(n1 * D // tn,),
        in_specs=[pl.BlockSpec((2 * n2, 2 * n2), lambda i: (0, 0)),
                  pl.BlockSpec((2 * n2, tn), lambda i: (0, i))],
        out_specs=pl.BlockSpec((2 * n2, tn), lambda i: (0, i)),
        out_shape=jax.ShapeDtypeStruct((2 * n2, n1 * D), BF16),
        compiler_params=_params(("parallel",), 2 * 2 * n2 * tn * 2, 2 * n2 * tn * 4),
        name="fourier_stage2",
    )(tab2, y.reshape(2 * n2, n1 * D))
    p = z.reshape(2, L, D)
    return pl.pallas_call(
        _fourier_out_body,
        grid=(L // tm,),
        in_specs=[pl.BlockSpec((None, tm, D), lambda i: (0, i, 0)),
                  pl.BlockSpec((None, tm, D), lambda i: (1, i, 0)),
                  pl.BlockSpec((GROUP_DIM, GROUP_DIM), lambda i: (0, 0)),
                  pl.BlockSpec((GROUP_DIM, GROUP_DIM), lambda i: (0, 0)),
                  pl.BlockSpec((D, D), lambda i: (0, 0)),
                  pl.BlockSpec((1, D), lambda i: (0, 0)),
                  pl.BlockSpec((tm, D), lambda i: (i, 0))],
        out_specs=pl.BlockSpec((tm, D), lambda i: (i, 0)),
        out_shape=jax.ShapeDtypeStruct((L, D), F32),
        compiler_params=_params(("parallel",), 2 * tm * D * 2, D * D * 2, 2 * tm * D * 4),
        name="fourier_out_proj",
    )(p, p, cc, sc, wout, bout, x)


def _kh_column_order():
    cols = []
    for h in range(N_K_HEADS):
        cols += list(range(h * HEAD_DIM, (h + 1) * HEAD_DIM))
        cols += list(range(KEY_DIM + h * HEAD_DIM, KEY_DIM + (h + 1) * HEAD_DIM))
        cols += list(range(2 * KEY_DIM + 2 * h * HEAD_DIM, 2 * KEY_DIM + (2 * h + 2) * HEAD_DIM))
    return jnp.asarray(cols, jnp.int32)


def _gate_column_order():
    cols = []
    for h in range(N_K_HEADS):
        units = [d * N_V_HEADS + 2 * h + s for d in range(N_DIRS) for s in range(2)]
        cols += [N_DIRS * N_V_HEADS + un for un in units] + units
    return jnp.asarray(cols, jnp.int32)


def _unit_rows(p):
    per = p.reshape(N_DIRS, N_K_HEADS, 2).transpose(1, 0, 2).reshape(N_K_HEADS, 2 * N_DIRS)
    per = jnp.concatenate([per, jnp.zeros_like(per)], axis=1)
    return jnp.broadcast_to(per[:, :, None], (N_K_HEADS, N_GATE_ROWS, LANES_V7X)).astype(F32)


def _deltanet_layer(x, nw, w_in, conv_w, a_log, dt_bias, out_norm_w, w_out):
    L = x.shape[0]
    nc = L // CHUNK
    order = _kh_column_order()
    gate0 = CONV_DIM + VALUE_DIM
    wqkv = w_in[:, :CONV_DIM][:, order].astype(BF16)
    wz = w_in[:, CONV_DIM:gate0].astype(BF16)
    n_gate = 2 * N_DIRS * N_V_HEADS
    wg = jnp.pad(w_in[:, gate0:][:, _gate_column_order()], ((0, 0), (0, LANES_V7X - n_gate))).astype(BF16)
    cw = jnp.pad(conv_w[:, order], ((0, SUBLANES_V7X - CONV_WIDTH), (0, 0)))

    qkv, z, gates = _in_proj(x, nw, wqkv, wz, wg)
    gr = gates[:, :n_gate].reshape(nc, CHUNK, N_K_HEADS, N_GATE_ROWS).transpose(2, 0, 3, 1)
    gr = jnp.concatenate([gr, gr], axis=-1)
    w, u, qd, qk, kdt, gl = _dn_prep(qkv, cw, gr, _unit_rows(a_log), _unit_rows(dt_bias))
    nu = N_DIRS * N_K_HEADS
    o = _dn_scan(w.reshape(nu, L, -1), u.reshape(nu, L, -1), qd.reshape(nu, L, -1), qk.reshape(nu, L, -1),
                 kdt.reshape(nu, nc * 2 * CHUNK, -1), gl.reshape(nu, nc, -1))
    return _dn_out(o, z, out_norm_w.reshape(1, -1), w_out.astype(BF16), x)


def kernel(x, mix_norm_w, ffn_norm_w, dn_w_in, dn_conv_w, dn_a_log, dn_dt_bias, dn_out_norm_w, dn_w_out,
           fn_w_out, fn_b_out, ffn_w_gate_up, ffn_w_down, final_norm_w):
    B, L, D = x.shape
    row = lambda v: v.reshape(1, -1)
    outs = []
    for b in range(B):
        h = x[b]
        h = _deltanet_layer(h, row(mix_norm_w[0]), dn_w_in[0], dn_conv_w[0], dn_a_log[0], dn_dt_bias[0],
                            dn_out_norm_w[0], dn_w_out[0])
        a = _ffn_up(h, row(ffn_norm_w[0]), ffn_w_gate_up[0].astype(BF16))
        h = _ffn_down(a, ffn_w_down[0].astype(BF16), h, row(final_norm_w), final_norm=False)
        h = _fourier_mixer(h, row(mix_norm_w[1]), fn_w_out[0].astype(BF16), row(fn_b_out[0]))
        a = _ffn_up(h, row(ffn_norm_w[1]), ffn_w_gate_up[1].astype(BF16))
        h = _ffn_down(a, ffn_w_down[1].astype(BF16), h, row(final_norm_w), final_norm=True)
        outs.append(h)
    return jnp.stack(outs, axis=0)
```

```python
import functools
import math

import jax
import jax.numpy as jnp
from jax import lax
from jax.experimental import pallas as pl
from jax.experimental.pallas import tpu as pltpu

F32 = jnp.float32
BF16 = jnp.bfloat16

LANES_V7X = 128
SUBLANES_V7X = 8
VMEM_BYTES_V7X = 64 * 1024 * 1024
VMEM_LIMIT_CAP = VMEM_BYTES_V7X - 8 * 1024 * 1024

D_MODEL = 1024
N_K_HEADS = 8
N_V_HEADS = 16
HEAD_DIM = 128
KEY_DIM = N_K_HEADS * HEAD_DIM
VALUE_DIM = N_V_HEADS * HEAD_DIM
CONV_DIM = 2 * KEY_DIM + VALUE_DIM
CONV_WIDTH = 5
CONV_HALO = SUBLANES_V7X
CHUNK = 64
N_DIRS = 2
KH_COLS = 2 * HEAD_DIM + 2 * HEAD_DIM
N_GATE_ROWS = 8
N_FOURIER_GROUPS = 4
GROUP_DIM = D_MODEL // N_FOURIER_GROUPS
D_FF = 2816
RMS_EPS = 1e-6
L2_EPS = 1e-6

DN_TILE = 512
SCAN_GROUP = 4


def _params(semantics, *block_bytes):
    need = 2 * sum(block_bytes) + 24 * 1024 * 1024
    return pltpu.CompilerParams(dimension_semantics=semantics,
                                vmem_limit_bytes=int(min(need, VMEM_LIMIT_CAP)))


def _rms(x, w):
    return x * lax.rsqrt(jnp.mean(x * x, axis=-1, keepdims=True) + RMS_EPS) * w


def _silu(x):
    return x * jax.nn.sigmoid(x)


def _softplus(x):
    return jnp.maximum(x, 0.0) + jnp.log1p(jnp.exp(-jnp.abs(x)))


def _dot(a, b):
    return jnp.dot(a, b, preferred_element_type=F32)


def _dot_nt(a, b):
    return lax.dot_general(a, b, (((1,), (1,)), ((), ())), preferred_element_type=F32)


def _inproj_body(x_ref, nw_ref, wqkv_ref, wz_ref, wg_ref, qkv_ref, z_ref, g_ref, h_ref):
    @pl.when(pl.program_id(1) == 0)
    def _():
        h = _rms(x_ref[...], nw_ref[...]).astype(BF16)
        h_ref[...] = h
        g_ref[...] = _dot_nt(wg_ref[...], h)

    h = h_ref[...]
    qkv_ref[...] = _dot(h, wqkv_ref[...])
    z_ref[...] = _dot(h, wz_ref[...]).astype(BF16)


def _in_proj(x, nw, wqkv, wz, wg, tm=1024, nj=4):
    L, D = x.shape
    tq, tz = wqkv.shape[1] // nj, wz.shape[1] // nj
    return pl.pallas_call(
        _inproj_body,
        grid=(L // tm, nj),
        in_specs=[pl.BlockSpec((tm, D), lambda i, j: (i, 0)),
                  pl.BlockSpec((1, D), lambda i, j: (0, 0)),
                  pl.BlockSpec((D, tq), lambda i, j: (0, j)),
                  pl.BlockSpec((D, tz), lambda i, j: (0, j)),
                  pl.BlockSpec((LANES_V7X, D), lambda i, j: (0, 0))],
        out_specs=[pl.BlockSpec((tm, tq), lambda i, j: (i, j)),
                   pl.BlockSpec((tm, tz), lambda i, j: (i, j)),
                   pl.BlockSpec((LANES_V7X, tm), lambda i, j: (0, i))],
        out_shape=[jax.ShapeDtypeStruct((L, wqkv.shape[1]), F32),
                   jax.ShapeDtypeStruct((L, wz.shape[1]), BF16),
                   jax.ShapeDtypeStruct((LANES_V7X, L), F32)],
        scratch_shapes=[pltpu.VMEM((tm, D), BF16)],
        compiler_params=_params(("parallel", "arbitrary"), tm * D * 4, D * tq * 2, D * tz * 2,
                                tm * tq * 4, tm * tz * 2, tm * LANES_V7X * 4, tm * D * 2),
        name="dn_in_proj",
    )(x, nw, wqkv, wz, wg)


def _block_diag4(x):
    blk = lax.broadcasted_iota(jnp.int32, x.shape, 1) // CHUNK
    zero = jnp.zeros_like(x)
    return jnp.concatenate([jnp.where(blk == r, x, zero) for r in range(4)], axis=0)


def _block_diag2(x):
    left = lax.broadcasted_iota(jnp.int32, x.shape, 1) < HEAD_DIM
    zero = jnp.zeros_like(x)
    return jnp.concatenate([jnp.where(left, x, zero), jnp.where(left, zero, x)], axis=0)


def _dn_prep_body(xm_ref, xp_ref, xn_ref, cw_ref, gr_ref, alog_ref, dtb_ref,
                  w_ref, u_ref, qd_ref, qk_ref, kdt_ref, gl_ref, xs_ref):
    i = pl.program_id(1)
    nt = pl.num_programs(1)
    T = xm_ref.shape[0]
    nch = T // CHUNK

    xs_ref[0:CONV_HALO, :] = jnp.where(i > 0, xp_ref[...], 0.0)
    xs_ref[CONV_HALO:CONV_HALO + T, :] = xm_ref[...]
    xs_ref[CONV_HALO + T:2 * CONV_HALO + T, :] = jnp.where(i < nt - 1, xn_ref[...], 0.0)
    first = CONV_HALO - (CONV_WIDTH - 1) // 2
    y = xs_ref[pl.ds(first, T), :] * cw_ref[0:1, :]
    for tap in range(1, CONV_WIDTH):
        y = y + xs_ref[pl.ds(first + tap, T), :] * cw_ref[tap:tap + 1, :]
    y = _silu(y)
    q = y[:, 0:HEAD_DIM]
    k = y[:, HEAD_DIM:2 * HEAD_DIM]
    v = y[:, 2 * HEAD_DIM:]
    q = q * lax.rsqrt(jnp.sum(q * q, axis=-1, keepdims=True) + L2_EPS) * (HEAD_DIM ** -0.5)
    k = k * lax.rsqrt(jnp.sum(k * k, axis=-1, keepdims=True) + L2_EPS)

    rows = nch * N_GATE_ROWS
    half = lax.broadcasted_iota(jnp.int32, (N_GATE_ROWS, LANES_V7X), 1) < CHUNK
    slabs = []
    for j in range(T // LANES_V7X):
        two = gr_ref[:, j * LANES_V7X:(j + 1) * LANES_V7X]
        swapped = pltpu.roll(two, CHUNK, 1)
        slabs += [jnp.where(half, two, swapped), jnp.where(half, swapped, two)]
    gr = jnp.concatenate(slabs, axis=0)
    alog = jnp.concatenate([alog_ref[...]] * nch, axis=0)
    dtb = jnp.concatenate([dtb_ref[...]] * nch, axis=0)
    g = -jnp.exp(alog) * _softplus(gr + dtb)
    beta = jax.nn.sigmoid(gr)
    lane = lax.broadcasted_iota(jnp.int32, (rows, LANES_V7X), 1)
    l64 = lane % CHUNK
    unit = lax.broadcasted_iota(jnp.int32, (rows, LANES_V7X), 0) % N_GATE_ROWS
    pre, suf = g, g
    step = 1
    while step < CHUNK:
        pre = pre + jnp.where(l64 >= step, pltpu.roll(pre, step, 1), 0.0)
        suf = suf + jnp.where(l64 < CHUNK - step, pltpu.roll(suf, LANES_V7X - step, 1), 0.0)
        step *= 2
    cum = jnp.where(unit < 2, pre, suf)
    tot = pre + suf - g
    ecum = jnp.exp(cum)
    edec = jnp.exp(tot - cum)
    gl = jnp.exp(tot)

    cb = jnp.where(unit < 4, cum, beta)
    assert 2 * rows == LANES_V7X
    cbt = jnp.transpose(jnp.concatenate([cb, cb], axis=0))
    ecbt = jnp.exp(cbt)

    def col(mat, c, r):
        idx = c * N_GATE_ROWS + r
        return jnp.broadcast_to(mat[0:CHUNK, idx:idx + 1], (CHUNK, LANES_V7X))

    def row(mat, c, r):
        idx = c * N_GATE_ROWS + r
        return mat[idx:idx + 1, :]

    ri = lax.broadcasted_iota(jnp.int32, (CHUNK, LANES_V7X), 0)
    li = lax.broadcasted_iota(jnp.int32, (CHUNK, LANES_V7X), 1)
    left = li < CHUNK
    li64 = li % CHUNK
    eye4 = jnp.concatenate([(ri == li64).astype(F32)] * 2, axis=1)

    sls = [slice(c * CHUNK, (c + 1) * CHUNK) for c in range(nch)]
    kbs = [k[sl].astype(BF16) for sl in sls]
    qkks = [_dot_nt(jnp.concatenate([q[sl].astype(BF16), kb], axis=0), jnp.concatenate([kb, kb], axis=0))
            for sl, kb in zip(sls, kbs)]

    pms, beta4, ecum4 = [], [], []
    for c, sl in enumerate(sls):
        qc = q[sl]
        qk2, kk = qkks[c][0:CHUNK], qkks[c][CHUNK:]
        kt2 = jnp.transpose(jnp.concatenate([k[sl], k[sl]], axis=0))
        n_pairs, beta_rows, ecum_rows = [], [], []
        for p in range(N_DIRS):
            r0, r1 = 2 * p, 2 * p + 1
            cum_row = jnp.where(left[0:1], row(cum, c, r0), row(cum, c, r1))
            cum_col = jnp.where(left, col(cbt, c, r0), col(cbt, c, r1))
            beta_col = jnp.where(left, col(cbt, c, 4 + r0), col(cbt, c, 4 + r1))
            incl = (ri >= li64) if p == 0 else (ri <= li64)
            strict = (ri > li64) if p == 0 else (ri < li64)
            dm = jnp.where(incl, jnp.exp(jnp.where(incl, cum_col - cum_row, 0.0)), 0.0)
            n_pairs.append(jnp.where(strict, kk * dm * beta_col, 0.0))
            qk_ref[p, sl, :] = (qk2 * dm).astype(BF16)
            qd_ref[p, sl, :] = jnp.concatenate(
                [qc * col(ecbt, c, r0), qc * col(ecbt, c, r1)], axis=1).astype(BF16)
            beta_rows.append(jnp.where(left[0:1], row(beta, c, 4 + r0), row(beta, c, 4 + r1)))
            ecum_rows.append(jnp.where(left[0:1], row(ecum, c, r0), row(ecum, c, r1)))
            edec_row = jnp.where(left[0:1], row(edec, c, r0), row(edec, c, r1))
            kdt_ref[p, c * 2 * CHUNK:(c + 1) * 2 * CHUNK, :] = (kt2 * edec_row).astype(BF16)
            gl_ref[p, c:c + 1, :] = jnp.concatenate([row(gl, c, r0), row(gl, c, r1)], axis=1)
        pms.append(-jnp.concatenate(n_pairs, axis=1))
        beta4.append(jnp.concatenate(beta_rows, axis=1))
        ecum4.append(jnp.concatenate(ecum_rows, axis=1))

    invs = [eye4 + pm for pm in pms]
    pbs = [pm.astype(BF16) for pm in pms]
    pws = [_dot(pb, _block_diag4(pb)) for pb in pbs]
    for _ in range(4):
        pbs = [pw.astype(BF16) for pw in pws]
        ress = [_dot(jnp.concatenate([inv.astype(BF16), pb], axis=0), _block_diag4(pb))
                for inv, pb in zip(invs, pbs)]
        invs = [inv + res[0:CHUNK] for inv, res in zip(invs, ress)]
        pws = [res[CHUNK:] for res in ress]
    invs = [inv + _dot(inv.astype(BF16), _block_diag4(pw.astype(BF16))) for inv, pw in zip(invs, pws)]

    for c, sl in enumerate(sls):
        t1 = invs[c] * beta4[c]
        t2 = t1 * ecum4[c]
        t1s = jnp.concatenate([t1[:, 0:LANES_V7X], t1[:, LANES_V7X:]], axis=0).astype(BF16)
        t2s = jnp.concatenate([t2[:, 0:LANES_V7X], t2[:, LANES_V7X:]], axis=0).astype(BF16)
        wv = _dot(t2s, _block_diag2(jnp.concatenate([kbs[c], kbs[c]], axis=1)))
        uv = _dot(t1s, _block_diag2(v[sl].astype(BF16)))
        for p in range(N_DIRS):
            w_ref[p, sl, :] = wv[p * CHUNK:(p + 1) * CHUNK].astype(BF16)
            u_ref[p, sl, :] = uv[p * CHUNK:(p + 1) * CHUNK].astype(BF16)


def _dn_prep(qkv, cw, gates_t, alog_b, dtb_b, T=DN_TILE):
    L = qkv.shape[0]
    nt, nch, nc = L // T, T // CHUNK, L // CHUNK
    hb = T // CONV_HALO
    last_halo = L // CONV_HALO - 1
    pair = 2 * HEAD_DIM
    out_shapes = [jax.ShapeDtypeStruct((N_DIRS, N_K_HEADS, L, pair), BF16),
                  jax.ShapeDtypeStruct((N_DIRS, N_K_HEADS, L, pair), BF16),
                  jax.ShapeDtypeStruct((N_DIRS, N_K_HEADS, L, pair), BF16),
                  jax.ShapeDtypeStruct((N_DIRS, N_K_HEADS, L, HEAD_DIM), BF16),
                  jax.ShapeDtypeStruct((N_DIRS, N_K_HEADS, nc * 2 * CHUNK, HEAD_DIM), BF16),
                  jax.ShapeDtypeStruct((N_DIRS, N_K_HEADS, nc, pair), F32)]
    big = lambda h, i: (0, h, i, 0)
    return pl.pallas_call(
        _dn_prep_body,
        grid=(N_K_HEADS, nt),
        in_specs=[pl.BlockSpec((T, KH_COLS), lambda h, i: (i, h)),
                  pl.BlockSpec((CONV_HALO, KH_COLS), lambda h, i: (jnp.maximum(i * hb - 1, 0), h)),
                  pl.BlockSpec((CONV_HALO, KH_COLS), lambda h, i: (jnp.minimum((i + 1) * hb, last_halo), h)),
                  pl.BlockSpec((SUBLANES_V7X, KH_COLS), lambda h, i: (0, h)),
                  pl.BlockSpec((N_GATE_ROWS, T), lambda h, i: (h, i)),
                  pl.BlockSpec((None, N_GATE_ROWS, LANES_V7X), lambda h, i: (h, 0, 0)),
                  pl.BlockSpec((None, N_GATE_ROWS, LANES_V7X), lambda h, i: (h, 0, 0))],
        out_specs=[pl.BlockSpec((N_DIRS, None, T, pair), big),
                   pl.BlockSpec((N_DIRS, None, T, pair), big),
                   pl.BlockSpec((N_DIRS, None, T, pair), big),
                   pl.BlockSpec((N_DIRS, None, T, HEAD_DIM), big),
                   pl.BlockSpec((N_DIRS, None, nch * 2 * CHUNK, HEAD_DIM), big),
                   pl.BlockSpec((N_DIRS, None, nch, pair), big)],
        out_shape=out_shapes,
        scratch_shapes=[pltpu.VMEM((T + 2 * CONV_HALO, KH_COLS), F32)],
        compiler_params=_params(("parallel", "parallel"), T * KH_COLS * 4, 3 * N_DIRS * T * pair * 2,
                                N_DIRS * T * HEAD_DIM * 2 * 2),
        name="dn_chunk_prep",
    )(qkv, qkv, qkv, cw, gates_t, alog_b, dtb_b)


def _dn_scan_body(w_ref, u_ref, qd_ref, qk_ref, kdt_ref, gl_ref, o_ref, s_ref, *, n_groups):
    grp = pl.program_id(0)
    bwd = grp >= n_groups // N_DIRS
    G, T = w_ref.shape[0], w_ref.shape[1]
    nch = T // CHUNK

    @pl.when(pl.program_id(1) == 0)
    def _():
        s_ref[...] = jnp.zeros_like(s_ref)

    states = [s_ref[g] for g in range(G)]
    for c in range(nch):
        ce = jnp.where(bwd, nch - 1 - c, c)
        r0 = pl.multiple_of(ce * CHUNK, CHUNK)
        r1 = pl.multiple_of(ce * 2 * CHUNK, 2 * CHUNK)
        for g in range(G):
            s = states[g]
            rows = pl.ds(r0, CHUNK)
            res_a = _dot(jnp.concatenate([w_ref[g, rows, :], qd_ref[g, rows, :]], axis=0),
                         _block_diag2(s.astype(BF16)))
            v_new = u_ref[g, rows, :].astype(F32) - res_a[0:CHUNK]
            res_b = _dot(jnp.concatenate([qk_ref[g, rows, :], kdt_ref[g, pl.ds(r1, 2 * CHUNK), :]], axis=0),
                         _block_diag2(v_new.astype(BF16)))
            o_ref[rows, g * 2 * HEAD_DIM:(g + 1) * 2 * HEAD_DIM] = res_a[CHUNK:] + res_b[0:CHUNK]
            states[g] = s * gl_ref[g, pl.ds(ce, 1), :] + res_b[CHUNK:]
    for g in range(G):
        s_ref[g] = states[g]


def _dn_scan(w, u, qd, qk, kdt, gl, T=DN_TILE, G=SCAN_GROUP):
    nu, L, pair = w.shape
    nt, nch = L // T, T // CHUNK
    n_groups = nu // G
    per_dir = n_groups // N_DIRS

    def tile(gi, i):
        return jnp.where(gi >= per_dir, nt - 1 - i, i)

    big = lambda gi, i: (gi, tile(gi, i), 0)
    return pl.pallas_call(
        functools.partial(_dn_scan_body, n_groups=n_groups),
        grid=(n_groups, nt),
        in_specs=[pl.BlockSpec((G, T, pair), big),
                  pl.BlockSpec((G, T, pair), big),
                  pl.BlockSpec((G, T, pair), big),
                  pl.BlockSpec((G, T, HEAD_DIM), big),
                  pl.BlockSpec((G, nch * 2 * CHUNK, HEAD_DIM), big),
                  pl.BlockSpec((G, nch, pair), big)],
        out_specs=pl.BlockSpec((None, T, G * pair), lambda gi, i: (gi // per_dir, tile(gi, i), gi % per_dir)),
        out_shape=jax.ShapeDtypeStruct((N_DIRS, L, VALUE_DIM), F32),
        scratch_shapes=[pltpu.VMEM((G, HEAD_DIM, pair), F32)],
        compiler_params=_params(("parallel", "arbitrary"), 3 * G * T * pair * 2, 2 * G * T * HEAD_DIM * 2,
                                T * G * pair * 4),
        name="dn_state_scan",
    )(w, u, qd, qk, kdt, gl)


def _dn_out_body(of_ref, ob_ref, z_ref, nw_ref, wout_ref, res_ref, out_ref):
    nw = nw_ref[...]
    pieces = []
    for h in range(N_V_HEADS):
        sl = slice(h * HEAD_DIM, (h + 1) * HEAD_DIM)
        o = of_ref[:, sl] + ob_ref[:, sl]
        y = _rms(o, nw) * _silu(z_ref[:, sl].astype(F32))
        pieces.append(y.astype(BF16))
    out_ref[...] = res_ref[...] + _dot(jnp.concatenate(pieces, axis=1), wout_ref[...])


def _dn_out(o, z, nw, wout, res, tm=512):
    L, D = res.shape
    V = z.shape[1]
    return pl.pallas_call(
        _dn_out_body,
        grid=(L // tm,),
        in_specs=[pl.BlockSpec((None, tm, V), lambda i: (0, i, 0)),
                  pl.BlockSpec((None, tm, V), lambda i: (1, i, 0)),
                  pl.BlockSpec((tm, V), lambda i: (i, 0)),
                  pl.BlockSpec((1, HEAD_DIM), lambda i: (0, 0)),
                  pl.BlockSpec((V, D), lambda i: (0, 0)),
                  pl.BlockSpec((tm, D), lambda i: (i, 0))],
        out_specs=pl.BlockSpec((tm, D), lambda i: (i, 0)),
        out_shape=jax.ShapeDtypeStruct((L, D), F32),
        compiler_params=_params(("parallel",), 2 * tm * V * 4, tm * V * 2, V * D * 2, 2 * tm * D * 4),
        name="dn_out_proj",
    )(o, o, z, nw, wout, res)


def _ffn_up_body(x_ref, nw_ref, wg_ref, wu_ref, a_ref, h_ref):
    @pl.when(pl.program_id(1) == 0)
    def _():
        h_ref[...] = _rms(x_ref[...], nw_ref[...]).astype(BF16)

    h = h_ref[...]
    a_ref[...] = (_silu(_dot(h, wg_ref[...])) * _dot(h, wu_ref[...])).astype(BF16)


def _ffn_up(x, nw, wgu, tm=1024, nj=2):
    L, D = x.shape
    dff = wgu.shape[1] // 2
    tn = dff // nj
    return pl.pallas_call(
        _ffn_up_body,
        grid=(L // tm, nj),
        in_specs=[pl.BlockSpec((tm, D), lambda i, j: (i, 0)),
                  pl.BlockSpec((1, D), lambda i, j: (0, 0)),
                  pl.BlockSpec((D, tn), lambda i, j: (0, j)),
                  pl.BlockSpec((D, tn), lambda i, j: (0, nj + j))],
        out_specs=pl.BlockSpec((tm, tn), lambda i, j: (i, j)),
        out_shape=jax.ShapeDtypeStruct((L, dff), BF16),
        scratch_shapes=[pltpu.VMEM((tm, D), BF16)],
        compiler_params=_params(("parallel", "arbitrary"), tm * D * 4, 2 * D * tn * 2, tm * tn * 2,
                                tm * tn * 4),
        name="ffn_gate_up",
    )(x, nw, wgu, wgu)


def _ffn_down_body(a_ref, w_ref, res_ref, nw_ref, o_ref, *, final_norm):
    y = res_ref[...] + _dot(a_ref[...], w_ref[...])
    o_ref[...] = _rms(y, nw_ref[...]) if final_norm else y


def _ffn_down(a, w, res, nw, final_norm, tm=512):
    L, D = res.shape
    dff = a.shape[1]
    return pl.pallas_call(
        functools.partial(_ffn_down_body, final_norm=final_norm),
        grid=(L // tm,),
        in_specs=[pl.BlockSpec((tm, dff), lambda i: (i, 0)),
                  pl.BlockSpec((dff, D), lambda i: (0, 0)),
                  pl.BlockSpec((tm, D), lambda i: (i, 0)),
                  pl.BlockSpec((1, D), lambda i: (0, 0))],
        out_specs=pl.BlockSpec((tm, D), lambda i: (i, 0)),
        out_shape=jax.ShapeDtypeStruct((L, D), F32),
        compiler_params=_params(("parallel",), tm * dff * 2, dff * D * 2, 2 * tm * D * 4),
        name="ffn_down",
    )(a, w, res, nw)


def _fourier1_body(x_ref, nw_ref, tab_ref, y_ref):
    nb = tab_ref.shape[0]
    n1 = x_ref.shape[0]
    D = nw_ref.shape[1]
    for j in range(nb):
        h = _rms(x_ref[:, j * D:(j + 1) * D], nw_ref[...]).astype(BF16)
        y = _dot(tab_ref[j], h)
        y_ref[0, j] = y[0:n1].astype(BF16)
        y_ref[1, j] = y[n1:].astype(BF16)


def _fourier2_body(f_ref, y_ref, z_ref):
    z_ref[...] = _dot(f_ref[...], y_ref[...]).astype(BF16)


def _fourier_out_body(pr_ref, pi_ref, cc_ref, sc_ref, w_ref, b_ref, res_ref, o_ref):
    pieces = []
    for g in range(N_FOURIER_GROUPS):
        sl = slice(g * GROUP_DIM, (g + 1) * GROUP_DIM)
        m = _dot(pr_ref[:, sl], cc_ref[...]) + _dot(pi_ref[:, sl], sc_ref[...])
        pieces.append(m.astype(BF16))
    o_ref[...] = res_ref[...] + (_dot(jnp.concatenate(pieces, axis=1), w_ref[...]) + b_ref[...])


def _dft_tables(L, n1, n2):
    two_pi = 2.0 * math.pi
    i2 = lax.broadcasted_iota(jnp.int32, (n2, n1, n1), 0)
    k1 = lax.broadcasted_iota(jnp.int32, (n2, n1, n1), 1)
    i1 = lax.broadcasted_iota(jnp.int32, (n2, n1, n1), 2)
    ang = ((k1 * (i2 + n2 * i1)) % L).astype(F32) * (two_pi / L)
    s1 = 1.0 / math.sqrt(n1)
    tab1 = jnp.concatenate([jnp.cos(ang) * s1, -jnp.sin(ang) * s1], axis=1).astype(BF16)
    a = lax.broadcasted_iota(jnp.int32, (n2, n2), 0)
    b = lax.broadcasted_iota(jnp.int32, (n2, n2), 1)
    ang2 = ((a * b) % n2).astype(F32) * (two_pi / n2)
    s2 = 1.0 / math.sqrt(n2)
    c2, sn2 = jnp.cos(ang2) * s2, jnp.sin(ang2) * s2
    tab2 = jnp.concatenate([jnp.concatenate([c2, sn2], axis=1),
                            jnp.concatenate([-sn2, c2], axis=1)], axis=0).astype(BF16)
    a = lax.broadcasted_iota(jnp.int32, (GROUP_DIM, GROUP_DIM), 0)
    b = lax.broadcasted_iota(jnp.int32, (GROUP_DIM, GROUP_DIM), 1)
    angc = ((a * b) % GROUP_DIM).astype(F32) * (two_pi / GROUP_DIM)
    sc = 1.0 / math.sqrt(GROUP_DIM)
    return tab1, tab2, (jnp.cos(angc) * sc).astype(BF16), (jnp.sin(angc) * sc).astype(BF16)


def _fourier_mixer(x, nw, wout, bout, nb=4, tm=512):
    L, D = x.shape
    n1 = n2 = math.isqrt(L)
    assert n1 * n2 == L
    tab1, tab2, cc, sc = _dft_tables(L, n1, n2)
    y = pl.pallas_call(
        _fourier1_body,
        grid=(n2 // nb,),
        in_specs=[pl.BlockSpec((n1, nb * D), lambda i: (0, i)),
                  pl.BlockSpec((1, D), lambda i: (0, 0)),
                  pl.BlockSpec((nb, 2 * n1, n1), lambda i: (i, 0, 0))],
        out_specs=pl.BlockSpec((2, nb, n1, D), lambda i: (0, i, 0, 0)),
        out_shape=jax.ShapeDtypeStruct((2, n2, n1, D), BF16),
        compiler_params=_params(("parallel",), n1 * nb * D * 4, nb * 2 * n1 * n1 * 2, 2 * nb * n1 * D * 2),
        name="fourier_stage1",
    )(x.reshape(n1, n2 * D), nw, tab1)
    tn = nb * D
    z = pl.pallas_call(
        _fourier2_body,
        grid=(n1 * D // tn,),
        in_specs=[pl.BlockSpec((2 * n2, 2 * n2), lambda i: (0, 0)),
                  pl.BlockSpec((2 * n2, tn), lambda i: (0, i))],
        out_specs=pl.BlockSpec((2 * n2, tn), lambda i: (0, i)),
        out_shape=jax.ShapeDtypeStruct((2 * n2, n1 * D), BF16),
        compiler_params=_params(("parallel",), 2 * 2 * n2 * tn * 2, 2 * n2 * tn * 4),
        name="fourier_stage2",
    )(tab2, y.reshape(2 * n2, n1 * D))
    p = z.reshape(2, L, D)
    return pl.pallas_call(
        _fourier_out_body,
        grid=(L // tm,),
        in_specs=[pl.BlockSpec((None, tm, D), lambda i: (0, i, 0)),
                  pl.BlockSpec((None, tm, D), lambda i: (1, i, 0)),
                  pl.BlockSpec((GROUP_DIM, GROUP_DIM), lambda i: (0, 0)),
                  pl.BlockSpec((GROUP_DIM, GROUP_DIM), lambda i: (0, 0)),
                  pl.BlockSpec((D, D), lambda i: (0, 0)),
                  pl.BlockSpec((1, D), lambda i: (0, 0)),
                  pl.BlockSpec((tm, D), lambda i: (i, 0))],
        out_specs=pl.BlockSpec((tm, D), lambda i: (i, 0)),
        out_shape=jax.ShapeDtypeStruct((L, D), F32),
        compiler_params=_params(("parallel",), 2 * tm * D * 2, D * D * 2, 2 * tm * D * 4),
        name="fourier_out_proj",
    )(p, p, cc, sc, wout, bout, x)


def _kh_columns(a):
    lead = a.shape[:-1]
    q = a[..., :KEY_DIM].reshape(*lead, N_K_HEADS, HEAD_DIM)
    k = a[..., KEY_DIM:2 * KEY_DIM].reshape(*lead, N_K_HEADS, HEAD_DIM)
    v = a[..., 2 * KEY_DIM:].reshape(*lead, N_K_HEADS, 2 * HEAD_DIM)
    return jnp.concatenate([q, k, v], axis=-1).reshape(*lead, CONV_DIM)


def _gate_column_order():
    cols = []
    for h in range(N_K_HEADS):
        units = [d * N_V_HEADS + 2 * h + s for d in range(N_DIRS) for s in range(2)]
        cols += [N_DIRS * N_V_HEADS + un for un in units] + units
    return jnp.asarray(cols, jnp.int32)


def _unit_rows(p):
    per = p.reshape(N_DIRS, N_K_HEADS, 2).transpose(1, 0, 2).reshape(N_K_HEADS, 2 * N_DIRS)
    per = jnp.concatenate([per, jnp.zeros_like(per)], axis=1)
    return jnp.broadcast_to(per[:, :, None], (N_K_HEADS, N_GATE_ROWS, LANES_V7X)).astype(F32)


def _deltanet_layer(x, nw, w_in, conv_w, a_log, dt_bias, out_norm_w, w_out):
    L = x.shape[0]
    nc = L // CHUNK
    gate0 = CONV_DIM + VALUE_DIM
    wqkv = _kh_columns(w_in[:, :CONV_DIM]).astype(BF16)
    wz = w_in[:, CONV_DIM:gate0].astype(BF16)
    n_gate = 2 * N_DIRS * N_V_HEADS
    wg = jnp.pad(w_in[:, gate0:][:, _gate_column_order()].T, ((0, LANES_V7X - n_gate), (0, 0))).astype(BF16)
    cw = jnp.pad(_kh_columns(conv_w), ((0, SUBLANES_V7X - CONV_WIDTH), (0, 0)))

    qkv, z, gates = _in_proj(x, nw, wqkv, wz, wg)
    w, u, qd, qk, kdt, gl = _dn_prep(qkv, cw, gates, _unit_rows(a_log), _unit_rows(dt_bias))
    nu = N_DIRS * N_K_HEADS
    o = _dn_scan(w.reshape(nu, L, -1), u.reshape(nu, L, -1), qd.reshape(nu, L, -1), qk.reshape(nu, L, -1),
                 kdt.reshape(nu, nc * 2 * CHUNK, -1), gl.reshape(nu, nc, -1))
    return _dn_out(o, z, out_norm_w.reshape(1, -1), w_out.astype(BF16), x)


def kernel(x, mix_norm_w, ffn_norm_w, dn_w_in, dn_conv_w, dn_a_log, dn_dt_bias, dn_out_norm_w, dn_w_out,
           fn_w_out, fn_b_out, ffn_w_gate_up, ffn_w_down, final_norm_w):
    B, L, D = x.shape
    row = lambda v: v.reshape(1, -1)
    outs = []
    for b in range(B):
        h = x[b]
        h = _deltanet_layer(h, row(mix_norm_w[0]), dn_w_in[0], dn_conv_w[0], dn_a_log[0], dn_dt_bias[0],
                            dn_out_norm_w[0], dn_w_out[0])
        a = _ffn_up(h, row(ffn_norm_w[0]), ffn_w_gate_up[0].astype(BF16))
        h = _ffn_down(a, ffn_w_down[0].astype(BF16), h, row(final_norm_w), final_norm=False)
        h = _fourier_mixer(h, row(mix_norm_w[1]), fn_w_out[0].astype(BF16), row(fn_b_out[0]))
        a = _ffn_up(h, row(ffn_norm_w[1]), ffn_w_gate_up[1].astype(BF16))
        h = _ffn_down(a, ffn_w_down[1].astype(BF16), h, row(final_norm_w), final_norm=True)
        outs.append(h)
    return jnp.stack(outs, axis=0)
```

```python
import functools
import math

import jax
import jax.numpy as jnp
from jax import lax
from jax.experimental import pallas as pl
from jax.experimental.pallas import tpu as pltpu

F32 = jnp.float32
BF16 = jnp.bfloat16

LANES_V7X = 128
SUBLANES_V7X = 8
VMEM_BYTES_V7X = 64 * 1024 * 1024
VMEM_LIMIT_CAP = VMEM_BYTES_V7X - 8 * 1024 * 1024

D_MODEL = 1024
N_K_HEADS = 8
N_V_HEADS = 16
HEAD_DIM = 128
KEY_DIM = N_K_HEADS * HEAD_DIM
VALUE_DIM = N_V_HEADS * HEAD_DIM
CONV_DIM = 2 * KEY_DIM + VALUE_DIM
CONV_WIDTH = 5
CONV_HALO = SUBLANES_V7X
CHUNK = 64
N_DIRS = 2
KH_COLS = 2 * HEAD_DIM + 2 * HEAD_DIM
N_GATE_ROWS = 8
N_FOURIER_GROUPS = 4
GROUP_DIM = D_MODEL // N_FOURIER_GROUPS
D_FF = 2816
RMS_EPS = 1e-6
L2_EPS = 1e-6

PREP_TILE = 1024
DN_TILE = 512
SCAN_GROUP = 8


def _params(semantics, *block_bytes):
    need = 2 * sum(block_bytes) + 24 * 1024 * 1024
    return pltpu.CompilerParams(dimension_semantics=semantics,
                                vmem_limit_bytes=int(min(need, VMEM_LIMIT_CAP)))


def _rms(x, w):
    return x * lax.rsqrt(jnp.mean(x * x, axis=-1, keepdims=True) + RMS_EPS) * w


def _silu(x):
    return x * jax.nn.sigmoid(x)


def _softplus(x):
    return jnp.maximum(x, 0.0) + jnp.log1p(jnp.exp(-jnp.abs(x)))


def _dot(a, b):
    return jnp.dot(a, b, preferred_element_type=F32)


def _dot_nt(a, b):
    return lax.dot_general(a, b, (((1,), (1,)), ((), ())), preferred_element_type=F32)


def _inproj_body(x_ref, nw_ref, wqkv_ref, wz_ref, wg_ref, qkv_ref, z_ref, g_ref):
    h = _rms(x_ref[...], nw_ref[...]).astype(BF16)
    qkv_ref[...] = _dot(h, wqkv_ref[...])
    z_ref[...] = _dot(h, wz_ref[...]).astype(BF16)

    @pl.when(pl.program_id(1) == 0)
    def _():
        g_ref[...] = _dot_nt(wg_ref[...], h)


def _in_proj(x, nw, wqkv, wz, wg, tm=1024, nj=4):
    L, D = x.shape
    tq, tz = wqkv.shape[1] // nj, wz.shape[1] // nj
    return pl.pallas_call(
        _inproj_body,
        grid=(L // tm, nj),
        in_specs=[pl.BlockSpec((tm, D), lambda i, j: (i, 0)),
                  pl.BlockSpec((1, D), lambda i, j: (0, 0)),
                  pl.BlockSpec((D, tq), lambda i, j: (0, j)),
                  pl.BlockSpec((D, tz), lambda i, j: (0, j)),
                  pl.BlockSpec((LANES_V7X, D), lambda i, j: (0, 0))],
        out_specs=[pl.BlockSpec((tm, tq), lambda i, j: (i, j)),
                   pl.BlockSpec((tm, tz), lambda i, j: (i, j)),
                   pl.BlockSpec((LANES_V7X, tm), lambda i, j: (0, i))],
        out_shape=[jax.ShapeDtypeStruct((L, wqkv.shape[1]), F32),
                   jax.ShapeDtypeStruct((L, wz.shape[1]), BF16),
                   jax.ShapeDtypeStruct((LANES_V7X, L), F32)],
        compiler_params=_params(("parallel", "arbitrary"), tm * D * 4, D * tq * 2, D * tz * 2,
                                tm * tq * 4, tm * tz * 2, tm * LANES_V7X * 4),
        name="dn_in_proj",
    )(x, nw, wqkv, wz, wg)


def _block_diag4(x):
    blk = lax.broadcasted_iota(jnp.int32, x.shape, 1) // CHUNK
    zero = jnp.zeros_like(x)
    return jnp.concatenate([jnp.where(blk == r, x, zero) for r in range(4)], axis=0)


def _block_diag2(x):
    left = lax.broadcasted_iota(jnp.int32, x.shape, 1) < HEAD_DIM
    zero = jnp.zeros_like(x)
    return jnp.concatenate([jnp.where(left, x, zero), jnp.where(left, zero, x)], axis=0)


def _dn_prep_body(xm_ref, xp_ref, xn_ref, cw_ref, gr_ref, alog_ref, dtb_ref,
                  w_ref, u_ref, qd_ref, qk_ref, kdt_ref, gl_ref, xs_ref, y_ref):
    s = pl.program_id(1)
    nt = pl.num_programs(1) - 1
    T = xm_ref.shape[0]
    nch = T // CHUNK
    conv_tile = jnp.minimum(s, nt - 1)
    ynext = y_ref.at[s % 2]
    ycur = y_ref.at[(s + 1) % 2]

    @pl.when(s == 0)
    def _():
        y_ref[1] = jnp.zeros(y_ref.shape[1:], F32)

    xs_ref[0:CONV_HALO, :] = jnp.where(conv_tile > 0, xp_ref[...], 0.0)
    xs_ref[CONV_HALO:CONV_HALO + T, :] = xm_ref[...]
    xs_ref[CONV_HALO + T:2 * CONV_HALO + T, :] = jnp.where(conv_tile < nt - 1, xn_ref[...], 0.0)
    first = CONV_HALO - (CONV_WIDTH - 1) // 2

    def conv_piece(c):
        base = first + c * CHUNK
        y = xs_ref[pl.ds(base, CHUNK), :] * cw_ref[0:1, :]
        for tap in range(1, CONV_WIDTH):
            y = y + xs_ref[pl.ds(base + tap, CHUNK), :] * cw_ref[tap:tap + 1, :]
        y = _silu(y)
        q = y[:, 0:HEAD_DIM]
        k = y[:, HEAD_DIM:2 * HEAD_DIM]
        q = q * lax.rsqrt(jnp.sum(q * q, axis=-1, keepdims=True) + L2_EPS) * (HEAD_DIM ** -0.5)
        k = k * lax.rsqrt(jnp.sum(k * k, axis=-1, keepdims=True) + L2_EPS)
        ynext[c * CHUNK:(c + 1) * CHUNK, :] = jnp.concatenate([q, k, y[:, 2 * HEAD_DIM:]], axis=1)

    n_slots = 8

    def conv_slot(slot):
        for c in range(slot * nch // n_slots, (slot + 1) * nch // n_slots):
            conv_piece(c)

    sls = [slice(c * CHUNK, (c + 1) * CHUNK) for c in range(nch)]
    q_of =lambda c: ycur[sls[c], 0:HEAD_DIM]
    k_of = lambda c: ycur[sls[c], HEAD_DIM:2 * HEAD_DIM]
    v_of = lambda c: ycur[sls[c], 2 * HEAD_DIM:]

    rows = nch * N_GATE_ROWS
    half = lax.broadcasted_iota(jnp.int32, (N_GATE_ROWS, LANES_V7X), 1) < CHUNK
    slabs = []
    for j in range(T // LANES_V7X):
        two = gr_ref[:, j * LANES_V7X:(j + 1) * LANES_V7X]
        swapped = pltpu.roll(two, CHUNK, 1)
        slabs += [jnp.where(half, two, swapped), jnp.where(half, swapped, two)]
    gr = jnp.concatenate(slabs, axis=0)
    alog = jnp.concatenate([alog_ref[...]] * nch, axis=0)
    dtb = jnp.concatenate([dtb_ref[...]] * nch, axis=0)
    g = -jnp.exp(alog) * _softplus(gr + dtb)
    beta = jax.nn.sigmoid(gr)
    lane = lax.broadcasted_iota(jnp.int32, (rows, LANES_V7X), 1)
    l64 = lane % CHUNK
    unit = lax.broadcasted_iota(jnp.int32, (rows, LANES_V7X), 0) % N_GATE_ROWS
    pre, suf = g, g
    step = 1
    while step < CHUNK:
        pre = pre + jnp.where(l64 >= step, pltpu.roll(pre, step, 1), 0.0)
        suf = suf + jnp.where(l64 < CHUNK - step, pltpu.roll(suf, LANES_V7X - step, 1), 0.0)
        step *= 2
    cum = jnp.where(unit < 2, pre, suf)
    tot = pre + suf - g
    ecum = jnp.exp(cum)
    edec = jnp.exp(tot - cum)
    gl = jnp.exp(tot)

    cb = jnp.where(unit < 4, cum, beta)
    assert rows == LANES_V7X
    cbt = jnp.transpose(cb)
    ecbt = jnp.exp(cbt)

    def col(mat, c, r):
        idx = c * N_GATE_ROWS + r
        return jnp.broadcast_to(mat[0:CHUNK, idx:idx + 1], (CHUNK, LANES_V7X))

    def row(mat, c, r):
        idx = c * N_GATE_ROWS + r
        return mat[idx:idx + 1, :]

    ri = lax.broadcasted_iota(jnp.int32, (CHUNK, LANES_V7X), 0)
    li = lax.broadcasted_iota(jnp.int32, (CHUNK, LANES_V7X), 1)
    left = li < CHUNK
    li64 = li % CHUNK
    eye4 = jnp.concatenate([(ri == li64).astype(F32)] * 2, axis=1)

    kbs = [k_of(c).astype(BF16) for c in range(nch)]
    qkks = [_dot_nt(jnp.concatenate([q_of(c).astype(BF16), kbs[c]], axis=0),
                    jnp.concatenate([kbs[c], kbs[c]], axis=0)) for c in range(nch)]
    conv_slot(0)

    pms, beta4, ecum4 = [], [], []
    for c, sl in enumerate(sls):
        qc, kc = q_of(c), k_of(c)
        qk2, kk = qkks[c][0:CHUNK], qkks[c][CHUNK:]
        kt2 = jnp.transpose(jnp.concatenate([kc, kc], axis=0))
        n_pairs, beta_rows, ecum_rows = [], [], []
        for p in range(N_DIRS):
            r0, r1 = 2 * p, 2 * p + 1
            cum_row = jnp.where(left[0:1], row(cum, c, r0), row(cum, c, r1))
            cum_col = jnp.where(left, col(cbt, c, r0), col(cbt, c, r1))
            beta_col = jnp.where(left, col(cbt, c, 4 + r0), col(cbt, c, 4 + r1))
            incl = (ri >= li64) if p == 0 else (ri <= li64)
            strict = (ri > li64) if p == 0 else (ri < li64)
            dm = jnp.where(incl, jnp.exp(jnp.where(incl, cum_col - cum_row, 0.0)), 0.0)
            n_pairs.append(jnp.where(strict, kk * dm * beta_col, 0.0))
            qk_ref[p, sl, :] = (qk2 * dm).astype(BF16)
            qd_ref[p, sl, :] = jnp.concatenate(
                [qc * col(ecbt, c, r0), qc * col(ecbt, c, r1)], axis=1).astype(BF16)
            beta_rows.append(jnp.where(left[0:1], row(beta, c, 4 + r0), row(beta, c, 4 + r1)))
            ecum_rows.append(jnp.where(left[0:1], row(ecum, c, r0), row(ecum, c, r1)))
            edec_row = jnp.where(left[0:1], row(edec, c, r0), row(edec, c, r1))
            kdt_ref[p, c * 2 * CHUNK:(c + 1) * 2 * CHUNK, :] = (kt2 * edec_row).astype(BF16)
            gl_ref[p, c:c + 1, :] = jnp.concatenate([row(gl, c, r0), row(gl, c, r1)], axis=1)
        pms.append(-jnp.concatenate(n_pairs, axis=1))
        beta4.append(jnp.concatenate(beta_rows, axis=1))
        ecum4.append(jnp.concatenate(ecum_rows, axis=1))

    invs = [eye4 + pm for pm in pms]
    pbs = [pm.astype(BF16) for pm in pms]
    pws = [_dot(pb, _block_diag4(pb)) for pb in pbs]
    conv_slot(1)
    for it in range(4):
        pbs = [pw.astype(BF16) for pw in pws]
        ress = [_dot(jnp.concatenate([inv.astype(BF16), pb], axis=0), _block_diag4(pb))
                for inv, pb in zip(invs, pbs)]
        invs = [inv + res[0:CHUNK] for inv, res in zip(invs, ress)]
        pws = [res[CHUNK:] for res in ress]
        conv_slot(2 + it)
    invs = [inv + _dot(inv.astype(BF16), _block_diag4(pw.astype(BF16))) for inv, pw in zip(invs, pws)]
    conv_slot(6)

    for c, sl in enumerate(sls):
        t1 = invs[c] * beta4[c]
        t2 = t1 * ecum4[c]
        t1s = jnp.concatenate([t1[:, 0:LANES_V7X], t1[:, LANES_V7X:]], axis=0).astype(BF16)
        t2s = jnp.concatenate([t2[:, 0:LANES_V7X], t2[:, LANES_V7X:]], axis=0).astype(BF16)
        wv = _dot(t2s, _block_diag2(jnp.concatenate([kbs[c], kbs[c]], axis=1)))
        uv = _dot(t1s, _block_diag2(v_of(c).astype(BF16)))
        for p in range(N_DIRS):
            w_ref[p, sl, :] = wv[p * CHUNK:(p + 1) * CHUNK].astype(BF16)
            u_ref[p, sl, :] = uv[p * CHUNK:(p + 1) * CHUNK].astype(BF16)
    conv_slot(7)


def _dn_prep(qkv, cw, gates_t, alog_b, dtb_b, T=PREP_TILE):
    L = qkv.shape[0]
    nt, nch, nc = L // T, T // CHUNK, L // CHUNK
    hb = T // CONV_HALO
    last_halo = L // CONV_HALO - 1
    pair = 2 * HEAD_DIM
    out_shapes = [jax.ShapeDtypeStruct((N_DIRS, N_K_HEADS, L, pair), BF16),
                  jax.ShapeDtypeStruct((N_DIRS, N_K_HEADS, L, pair), BF16),
                  jax.ShapeDtypeStruct((N_DIRS, N_K_HEADS, L, pair), BF16),
                  jax.ShapeDtypeStruct((N_DIRS, N_K_HEADS, L, HEAD_DIM), BF16),
                  jax.ShapeDtypeStruct((N_DIRS, N_K_HEADS, nc * 2 * CHUNK, HEAD_DIM), BF16),
                  jax.ShapeDtypeStruct((N_DIRS, N_K_HEADS, nc, pair), F32)]
    conv_tile = lambda s: jnp.minimum(s, nt - 1)
    prep_tile = lambda s: jnp.maximum(s - 1, 0)
    big = lambda h, s: (0, h, prep_tile(s), 0)
    return pl.pallas_call(
        _dn_prep_body,
        grid=(N_K_HEADS, nt + 1),
        in_specs=[pl.BlockSpec((T, KH_COLS), lambda h, s: (conv_tile(s), h)),
                  pl.BlockSpec((CONV_HALO, KH_COLS), lambda h, s: (jnp.maximum(conv_tile(s) * hb - 1, 0), h)),
                  pl.BlockSpec((CONV_HALO, KH_COLS),
                               lambda h, s: (jnp.minimum((conv_tile(s) + 1) * hb, last_halo), h)),
                  pl.BlockSpec((SUBLANES_V7X, KH_COLS), lambda h, s: (0, h)),
                  pl.BlockSpec((N_GATE_ROWS, T), lambda h, s: (h, prep_tile(s))),
                  pl.BlockSpec((None, N_GATE_ROWS, LANES_V7X), lambda h, i: (h, 0, 0)),
                  pl.BlockSpec((None, N_GATE_ROWS, LANES_V7X), lambda h, i: (h, 0, 0))],
        out_specs=[pl.BlockSpec((N_DIRS, None, T, pair), big),
                   pl.BlockSpec((N_DIRS, None, T, pair), big),
                   pl.BlockSpec((N_DIRS, None, T, pair), big),
                   pl.BlockSpec((N_DIRS, None, T, HEAD_DIM), big),
                   pl.BlockSpec((N_DIRS, None, nch * 2 * CHUNK, HEAD_DIM), big),
                   pl.BlockSpec((N_DIRS, None, nch, pair), big)],
        out_shape=out_shapes,
        scratch_shapes=[pltpu.VMEM((T + 2 * CONV_HALO, KH_COLS), F32),
                        pltpu.VMEM((2, T, KH_COLS), F32)],
        compiler_params=_params(("parallel", "arbitrary"), T * KH_COLS * 4, 3 * N_DIRS * T * pair * 2,
                                N_DIRS * T * HEAD_DIM * 2 * 2, T * KH_COLS * 4),
        name="dn_chunk_prep",
    )(qkv, qkv, qkv, cw, gates_t, alog_b, dtb_b)


def _dn_scan_body(w_ref, u_ref, qd_ref, qk_ref, kdt_ref, gl_ref, o_ref, s_ref, *, n_groups):
    grp = pl.program_id(0)
    bwd = grp >= n_groups // N_DIRS
    G, T = w_ref.shape[0], w_ref.shape[1]
    nch = T // CHUNK

    @pl.when(pl.program_id(1) == 0)
    def _():
        s_ref[...] = jnp.zeros_like(s_ref)

    states = [s_ref[g] for g in range(G)]
    for c in range(nch):
        ce = jnp.where(bwd, nch - 1 - c, c)
        r0 = pl.multiple_of(ce * CHUNK, CHUNK)
        r1 = pl.multiple_of(ce * 2 * CHUNK, 2 * CHUNK)
        rows = pl.ds(r0, CHUNK)
        res_a = [_dot(jnp.concatenate([w_ref[g, rows, :], qd_ref[g, rows, :]], axis=0),
                      _block_diag2(states[g].astype(BF16))) for g in range(G)]
        v_new = [u_ref[g, rows, :].astype(F32) - res_a[g][0:CHUNK] for g in range(G)]
        res_b = [_dot(jnp.concatenate([qk_ref[g, rows, :], kdt_ref[g, pl.ds(r1, 2 * CHUNK), :]], axis=0),
                      _block_diag2(v_new[g].astype(BF16))) for g in range(G)]
        for g in range(G):
            o_ref[rows, g * 2 * HEAD_DIM:(g + 1) * 2 * HEAD_DIM] = res_a[g][CHUNK:] + res_b[g][0:CHUNK]
            states[g] = states[g] * gl_ref[g, pl.ds(ce, 1), :] + res_b[g][CHUNK:]
    for g in range(G):
        s_ref[g] = states[g]


def _dn_scan(w, u, qd, qk, kdt, gl, T=DN_TILE, G=SCAN_GROUP):
    nu, L, pair = w.shape
    nt, nch = L // T, T // CHUNK
    n_groups = nu // G
    per_dir = n_groups // N_DIRS

    def tile(gi, i):
        return jnp.where(gi >= per_dir, nt - 1 - i, i)

    big = lambda gi, i: (gi, tile(gi, i), 0)
    return pl.pallas_call(
        functools.partial(_dn_scan_body, n_groups=n_groups),
        grid=(n_groups, nt),
        in_specs=[pl.BlockSpec((G, T, pair), big),
                  pl.BlockSpec((G, T, pair), big),
                  pl.BlockSpec((G, T, pair), big),
                  pl.BlockSpec((G, T, HEAD_DIM), big),
                  pl.BlockSpec((G, nch * 2 * CHUNK, HEAD_DIM), big),
                  pl.BlockSpec((G, nch, pair), big)],
        out_specs=pl.BlockSpec((None, T, G * pair), lambda gi, i: (gi // per_dir, tile(gi, i), gi % per_dir)),
        out_shape=jax.ShapeDtypeStruct((N_DIRS, L, VALUE_DIM), F32),
        scratch_shapes=[pltpu.VMEM((G, HEAD_DIM, pair), F32)],
        compiler_params=_params(("parallel", "arbitrary"), 3 * G * T * pair * 2, 2 * G * T * HEAD_DIM * 2,
                                T * G * pair * 4),
        name="dn_state_scan",
    )(w, u, qd, qk, kdt, gl)


def _dn_out_body(of_ref, ob_ref, z_ref, nw_ref, wout_ref, res_ref, out_ref):
    nw = nw_ref[...]
    pieces = []
    for h in range(N_V_HEADS):
        sl = slice(h * HEAD_DIM, (h + 1) * HEAD_DIM)
        o = of_ref[:, sl] + ob_ref[:, sl]
        y = _rms(o, nw) * _silu(z_ref[:, sl].astype(F32))
        pieces.append(y.astype(BF16))
    out_ref[...] = res_ref[...] + _dot(jnp.concatenate(pieces, axis=1), wout_ref[...])


def _dn_out(o, z, nw, wout, res, tm=512):
    L, D = res.shape
    V = z.shape[1]
    return pl.pallas_call(
        _dn_out_body,
        grid=(L // tm,),
        in_specs=[pl.BlockSpec((None, tm, V), lambda i: (0, i, 0)),
                  pl.BlockSpec((None, tm, V), lambda i: (1, i, 0)),
                  pl.BlockSpec((tm, V), lambda i: (i, 0)),
                  pl.BlockSpec((1, HEAD_DIM), lambda i: (0, 0)),
                  pl.BlockSpec((V, D), lambda i: (0, 0)),
                  pl.BlockSpec((tm, D), lambda i: (i, 0))],
        out_specs=pl.BlockSpec((tm, D), lambda i: (i, 0)),
        out_shape=jax.ShapeDtypeStruct((L, D), F32),
        compiler_params=_params(("parallel",), 2 * tm * V * 4, tm * V * 2, V * D * 2, 2 * tm * D * 4),
        name="dn_out_proj",
    )(o, o, z, nw, wout, res)


def _ffn_up_body(x_ref, nw_ref, wg_ref, wu_ref, a_ref):
    h = _rms(x_ref[...], nw_ref[...]).astype(BF16)
    a_ref[...] = (_silu(_dot(h, wg_ref[...])) * _dot(h, wu_ref[...])).astype(BF16)


def _ffn_up(x, nw, wgu, layer, tm=1024, nj=2):
    L, D = x.shape
    dff = wgu.shape[2] // 2
    tn = dff // nj
    return pl.pallas_call(
        _ffn_up_body,
        grid=(L // tm, nj),
        in_specs=[pl.BlockSpec((tm, D), lambda i, j: (i, 0)),
                  pl.BlockSpec((1, D), lambda i, j: (0, 0)),
                  pl.BlockSpec((None, D, tn), lambda i, j: (layer, 0, j)),
                  pl.BlockSpec((None, D, tn), lambda i, j: (layer, 0, nj + j))],
        out_specs=pl.BlockSpec((tm, tn), lambda i, j: (i, j)),
        out_shape=jax.ShapeDtypeStruct((L, dff), BF16),
        compiler_params=_params(("parallel", "arbitrary"), tm * D * 4, 2 * D * tn * 2, tm * tn * 2,
                                tm * tn * 4),
        name="ffn_gate_up",
    )(x, nw, wgu, wgu)


def _ffn_down_body(a_ref, w_ref, res_ref, nw_ref, o_ref, *, final_norm):
    y = res_ref[...] + _dot(a_ref[...], w_ref[...])
    o_ref[...] = _rms(y, nw_ref[...]) if final_norm else y


def _ffn_down(a, w, layer, res, nw, final_norm, tm=512):
    L, D = res.shape
    dff = a.shape[1]
    return pl.pallas_call(
        functools.partial(_ffn_down_body, final_norm=final_norm),
        grid=(L // tm,),
        in_specs=[pl.BlockSpec((tm, dff), lambda i: (i, 0)),
                  pl.BlockSpec((None, dff, D), lambda i: (layer, 0, 0)),
                  pl.BlockSpec((tm, D), lambda i: (i, 0)),
                  pl.BlockSpec((1, D), lambda i: (0, 0))],
        out_specs=pl.BlockSpec((tm, D), lambda i: (i, 0)),
        out_shape=jax.ShapeDtypeStruct((L, D), F32),
        compiler_params=_params(("parallel",), tm * dff * 2, dff * D * 2, 2 * tm * D * 4),
        name="ffn_down",
    )(a, w, res, nw)


def _fourier1_body(x_ref, nw_ref, tab_ref, y_ref):
    nb = tab_ref.shape[0]
    n1 = x_ref.shape[0]
    D = nw_ref.shape[1]
    for j in range(nb):
        h = _rms(x_ref[:, j * D:(j + 1) * D], nw_ref[...]).astype(BF16)
        y = _dot(tab_ref[j], h)
        y_ref[0, j] = y[0:n1].astype(BF16)
        y_ref[1, j] = y[n1:].astype(BF16)


def _fourier2_body(f_ref, y_ref, z_ref):
    z_ref[...] = _dot(f_ref[...], y_ref[...]).astype(BF16)


def _fourier_out_body(pr_ref, pi_ref, cc_ref, sc_ref, w_ref, b_ref, res_ref, o_ref):
    pieces = []
    for g in range(N_FOURIER_GROUPS):
        sl = slice(g * GROUP_DIM, (g + 1) * GROUP_DIM)
        m = _dot(pr_ref[:, sl], cc_ref[...]) + _dot(pi_ref[:, sl], sc_ref[...])
        pieces.append(m.astype(BF16))
    o_ref[...] = res_ref[...] + (_dot(jnp.concatenate(pieces, axis=1), w_ref[...]) + b_ref[...])


def _dft_tables(L, n1, n2):
    two_pi = 2.0 * math.pi
    i2 = lax.broadcasted_iota(jnp.int32, (n2, n1, n1), 0)
    k1 = lax.broadcasted_iota(jnp.int32, (n2, n1, n1), 1)
    i1 = lax.broadcasted_iota(jnp.int32, (n2, n1, n1), 2)
    ang = ((k1 * (i2 + n2 * i1)) % L).astype(F32) * (two_pi / L)
    s1 = 1.0 / math.sqrt(n1)
    tab1 = jnp.concatenate([jnp.cos(ang) * s1, -jnp.sin(ang) * s1], axis=1).astype(BF16)
    a = lax.broadcasted_iota(jnp.int32, (n2, n2), 0)
    b = lax.broadcasted_iota(jnp.int32, (n2, n2), 1)
    ang2 = ((a * b) % n2).astype(F32) * (two_pi / n2)
    s2 = 1.0 / math.sqrt(n2)
    c2, sn2 = jnp.cos(ang2) * s2, jnp.sin(ang2) * s2
    tab2 = jnp.concatenate([jnp.concatenate([c2, sn2], axis=1),
                            jnp.concatenate([-sn2, c2], axis=1)], axis=0).astype(BF16)
    a = lax.broadcasted_iota(jnp.int32, (GROUP_DIM, GROUP_DIM), 0)
    b = lax.broadcasted_iota(jnp.int32, (GROUP_DIM, GROUP_DIM), 1)
    angc = ((a * b) % GROUP_DIM).astype(F32) * (two_pi / GROUP_DIM)
    sc = 1.0 / math.sqrt(GROUP_DIM)
    return tab1, tab2, (jnp.cos(angc) * sc).astype(BF16), (jnp.sin(angc) * sc).astype(BF16)


def _fourier_mixer(x, nw, wout, bout, nb=4, tm=512):
    L, D = x.shape
    n1 = n2 = math.isqrt(L)
    assert n1 * n2 == L
    tab1, tab2, cc, sc = _dft_tables(L, n1, n2)
    y = pl.pallas_call(
        _fourier1_body,
        grid=(n2 // nb,),
        in_specs=[pl.BlockSpec((n1, nb * D), lambda i: (0, i)),
                  pl.BlockSpec((1, D), lambda i: (0, 0)),
                  pl.BlockSpec((nb, 2 * n1, n1), lambda i: (i, 0, 0))],
        out_specs=pl.BlockSpec((2, nb, n1, D), lambda i: (0, i, 0, 0)),
        out_shape=jax.ShapeDtypeStruct((2, n2, n1, D), BF16),
        compiler_params=_params(("parallel",), n1 * nb * D * 4, nb * 2 * n1 * n1 * 2, 2 * nb * n1 * D * 2),
        name="fourier_stage1",
    )(x.reshape(n1, n2 * D), nw, tab1)
    tn = nb * D
    z = pl.pallas_call(
        _fourier2_body,
        grid=(n1 * D // tn,),
        in_specs=[pl.BlockSpec((2 * n2, 2 * n2), lambda i: (0, 0)),
                  pl.BlockSpec((2 * n2, tn), lambda i: (0, i))],
        out_specs=pl.BlockSpec((2 * n2, tn), lambda i: (0, i)),
        out_shape=jax.ShapeDtypeStruct((2 * n2, n1 * D), BF16),
        compiler_params=_params(("parallel",), 2 * 2 * n2 * tn * 2, 2 * n2 * tn * 4),
        name="fourier_stage2",
    )(tab2, y.reshape(2 * n2, n1 * D))
    p = z.reshape(2, L, D)
    return pl.pallas_call(
        _fourier_out_body,
        grid=(L // tm,),
        in_specs=[pl.BlockSpec((None, tm, D), lambda i: (0, i, 0)),
                  pl.BlockSpec((None, tm, D), lambda i: (1, i, 0)),
                  pl.BlockSpec((GROUP_DIM, GROUP_DIM), lambda i: (0, 0)),
                  pl.BlockSpec((GROUP_DIM, GROUP_DIM), lambda i: (0, 0)),
                  pl.BlockSpec((D, D), lambda i: (0, 0)),
                  pl.BlockSpec((1, D), lambda i: (0, 0)),
                  pl.BlockSpec((tm, D), lambda i: (i, 0))],
        out_specs=pl.BlockSpec((tm, D), lambda i: (i, 0)),
        out_shape=jax.ShapeDtypeStruct((L, D), F32),
        compiler_params=_params(("parallel",), 2 * tm * D * 2, D * D * 2, 2 * tm * D * 4),
        name="fourier_out_proj",
    )(p, p, cc, sc, wout, bout, x)


def _kh_columns(a):
    lead = a.shape[:-1]
    q = a[..., :KEY_DIM].reshape(*lead, N_K_HEADS, HEAD_DIM)
    k = a[..., KEY_DIM:2 * KEY_DIM].reshape(*lead, N_K_HEADS, HEAD_DIM)
    v = a[..., 2 * KEY_DIM:].reshape(*lead, N_K_HEADS, 2 * HEAD_DIM)
    return jnp.concatenate([q, k, v], axis=-1).reshape(*lead, CONV_DIM)


def _gate_column_order():
    cols = []
    for h in range(N_K_HEADS):
        units = [d * N_V_HEADS + 2 * h + s for d in range(N_DIRS) for s in range(2)]
        cols += [N_DIRS * N_V_HEADS + un for un in units] + units
    return jnp.asarray(cols, jnp.int32)


def _unit_rows(p):
    per = p.reshape(N_DIRS, N_K_HEADS, 2).transpose(1, 0, 2).reshape(N_K_HEADS, 2 * N_DIRS)
    per = jnp.concatenate([per, jnp.zeros_like(per)], axis=1)
    return jnp.broadcast_to(per[:, :, None], (N_K_HEADS, N_GATE_ROWS, LANES_V7X)).astype(F32)


def _deltanet_layer(x, nw, w_in, conv_w, a_log, dt_bias, out_norm_w, w_out):
    L = x.shape[0]
    nc = L // CHUNK
    gate0 = CONV_DIM + VALUE_DIM
    wqkv = _kh_columns(w_in[:, :CONV_DIM]).astype(BF16)
    wz = w_in[:, CONV_DIM:gate0].astype(BF16)
    n_gate = 2 * N_DIRS * N_V_HEADS
    wg = jnp.pad(w_in[:, gate0:][:, _gate_column_order()].T, ((0, LANES_V7X - n_gate), (0, 0))).astype(BF16)
    cw = jnp.pad(_kh_columns(conv_w), ((0, SUBLANES_V7X - CONV_WIDTH), (0, 0)))

    qkv, z, gates = _in_proj(x, nw, wqkv, wz, wg)
    w, u, qd, qk, kdt, gl = _dn_prep(qkv, cw, gates, _unit_rows(a_log), _unit_rows(dt_bias))
    nu = N_DIRS * N_K_HEADS
    o = _dn_scan(w.reshape(nu, L, -1), u.reshape(nu, L, -1), qd.reshape(nu, L, -1), qk.reshape(nu, L, -1),
                 kdt.reshape(nu, nc * 2 * CHUNK, -1), gl.reshape(nu, nc, -1))
    return _dn_out(o, z, out_norm_w.reshape(1, -1), w_out.astype(BF16), x)


def kernel(x, mix_norm_w, ffn_norm_w, dn_w_in, dn_conv_w, dn_a_log, dn_dt_bias, dn_out_norm_w, dn_w_out,
           fn_w_out, fn_b_out, ffn_w_gate_up, ffn_w_down, final_norm_w):
    B, L, D = x.shape
    row = lambda v: v.reshape(1, -1)
    wgu, wdn = ffn_w_gate_up.astype(BF16), ffn_w_down.astype(BF16)
    outs = []
    for b in range(B):
        h = x[b]
        h = _deltanet_layer(h, row(mix_norm_w[0]), dn_w_in[0], dn_conv_w[0], dn_a_log[0], dn_dt_bias[0],
                            dn_out_norm_w[0], dn_w_out[0])
        a = _ffn_up(h, row(ffn_norm_w[0]), wgu, 0)
        h = _ffn_down(a, wdn, 0, h, row(final_norm_w), final_norm=False)
        h = _fourier_mixer(h, row(mix_norm_w[1]), fn_w_out[0].astype(BF16), row(fn_b_out[0]))
        a = _ffn_up(h, row(ffn_norm_w[1]), wgu, 1)
        h = _ffn_down(a, wdn, 1, h, row(final_norm_w), final_norm=True)
        outs.append(h)
    return jnp.stack(outs, axis=0)
```

```python
import functools
import math

import jax
import jax.numpy as jnp
from jax import lax
from jax.experimental import pallas as pl
from jax.experimental.pallas import tpu as pltpu

F32 = jnp.float32
BF16 = jnp.bfloat16

LANES_V7X = 128
SUBLANES_V7X = 8
VMEM_BYTES_V7X = 64 * 1024 * 1024
VMEM_LIMIT_CAP = VMEM_BYTES_V7X - 8 * 1024 * 1024

D_MODEL = 1024
N_K_HEADS = 8
N_V_HEADS = 16
HEAD_DIM = 128
KEY_DIM = N_K_HEADS * HEAD_DIM
VALUE_DIM = N_V_HEADS * HEAD_DIM
CONV_DIM = 2 * KEY_DIM + VALUE_DIM
CONV_WIDTH = 5
CONV_HALO = 16
CHUNK = 64
N_DIRS = 2
KH_COLS = 2 * HEAD_DIM + 2 * HEAD_DIM
N_GATE_ROWS = 8
N_FOURIER_GROUPS = 4
GROUP_DIM = D_MODEL // N_FOURIER_GROUPS
D_FF = 2816
RMS_EPS = 1e-6
L2_EPS = 1e-6

PREP_TILE = 1024
DN_TILE = 512
SCAN_GROUP = 8


def _params(semantics, *block_bytes):
    need = 2 * sum(block_bytes) + 24 * 1024 * 1024
    return pltpu.CompilerParams(dimension_semantics=semantics,
                                vmem_limit_bytes=int(min(need, VMEM_LIMIT_CAP)))


def _rms(x, w):
    return x * lax.rsqrt(jnp.mean(x * x, axis=-1, keepdims=True) + RMS_EPS) * w


def _silu(x):
    return x * jax.nn.sigmoid(x)


def _softplus(x):
    return jnp.maximum(x, 0.0) + jnp.log1p(jnp.exp(-jnp.abs(x)))


def _dot(a, b):
    return jnp.dot(a, b, preferred_element_type=F32)


def _dot_nt(a, b):
    return lax.dot_general(a, b, (((1,), (1,)), ((), ())), preferred_element_type=F32)


def _inproj_body(xm_ref, xp_ref, xn_ref, nw_ref, wqkv_ref, cw_ref, wz_ref, wg_ref,
                 qkv_ref, z_ref, g_ref, h_ref, p_ref):
    i, j = pl.program_id(0), pl.program_id(1)
    tm = xm_ref.shape[0]

    @pl.when(j == 0)
    def _():
        nw = nw_ref[...]
        hm = _rms(xm_ref[...], nw).astype(BF16)
        h_ref[0:CONV_HALO, :] = jnp.where(i > 0, _rms(xp_ref[...], nw), 0.0).astype(BF16)
        h_ref[CONV_HALO:CONV_HALO + tm, :] = hm
        h_ref[CONV_HALO + tm:, :] = jnp.where(i < pl.num_programs(0) - 1, _rms(xn_ref[...], nw), 0.0).astype(BF16)
        g_ref[...] = _dot_nt(wg_ref[...], hm)

    proj = _dot(h_ref[...], wqkv_ref[...])
    n_slabs = p_ref.shape[0]
    for c in range(n_slabs):
        p_ref[c] = proj[:, c * LANES_V7X:(c + 1) * LANES_V7X]
    first = CONV_HALO - (CONV_WIDTH - 1) // 2
    slabs_per_head = KH_COLS // LANES_V7X
    for c in range(n_slabs):
        lanes = slice(c * LANES_V7X, (c + 1) * LANES_V7X)
        y = p_ref[c, pl.ds(first, tm), :] * cw_ref[0:1, lanes]
        for tap in range(1, CONV_WIDTH):
            y = y + p_ref[c, pl.ds(first + tap, tm), :] * cw_ref[tap:tap + 1, lanes]
        y = _silu(y)
        role = c % slabs_per_head
        if role < 2:
            y = y * lax.rsqrt(jnp.sum(y * y, axis=-1, keepdims=True) + L2_EPS)
        if role == 0:
            y = y * (HEAD_DIM ** -0.5)
        qkv_ref[:, lanes] = y
    z_ref[...] = _dot(h_ref[CONV_HALO:CONV_HALO + tm, :], wz_ref[...]).astype(BF16)


def _in_proj(x, nw, wqkv, cw, wz, wg, tm=1024, nj=4):
    L, D = x.shape
    tq, tz = wqkv.shape[1] // nj, wz.shape[1] // nj
    hb = tm // CONV_HALO
    last_halo = L // CONV_HALO - 1
    return pl.pallas_call(
        _inproj_body,
        grid=(L // tm, nj),
        in_specs=[pl.BlockSpec((tm, D), lambda i, j: (i, 0)),
                  pl.BlockSpec((CONV_HALO, D), lambda i, j: (jnp.maximum(i * hb - 1, 0), 0)),
                  pl.BlockSpec((CONV_HALO, D), lambda i, j: (jnp.minimum((i + 1) * hb, last_halo), 0)),
                  pl.BlockSpec((1, D), lambda i, j: (0, 0)),
                  pl.BlockSpec((D, tq), lambda i, j: (0, j)),
                  pl.BlockSpec((SUBLANES_V7X, tq), lambda i, j: (0, j)),
                  pl.BlockSpec((D, tz), lambda i, j: (0, j)),
                  pl.BlockSpec((LANES_V7X, D), lambda i, j: (0, 0))],
        out_specs=[pl.BlockSpec((tm, tq), lambda i, j: (i, j)),
                   pl.BlockSpec((tm, tz), lambda i, j: (i, j)),
                   pl.BlockSpec((LANES_V7X, tm), lambda i, j: (0, i))],
        out_shape=[jax.ShapeDtypeStruct((L, wqkv.shape[1]), F32),
                   jax.ShapeDtypeStruct((L, wz.shape[1]), BF16),
                   jax.ShapeDtypeStruct((LANES_V7X, L), F32)],
        scratch_shapes=[pltpu.VMEM((tm + 2 * CONV_HALO, D), BF16),
                        pltpu.VMEM((tq // LANES_V7X, tm + 2 * CONV_HALO, LANES_V7X), F32)],
        compiler_params=_params(("parallel", "arbitrary"), tm * D * 4, D * tq * 2, D * tz * 2,
                                tm * tq * 4, tm * tz * 2, tm * LANES_V7X * 4, tm * D, tm * tq * 2),
        name="dn_in_proj",
    )(x, x, x, nw, wqkv, cw, wz, wg)


def _block_diag4(x):
    blk = lax.broadcasted_iota(jnp.int32, x.shape, 1) // CHUNK
    zero = jnp.zeros_like(x)
    return jnp.concatenate([jnp.where(blk == r, x, zero) for r in range(4)], axis=0)


def _block_diag2(x):
    left = lax.broadcasted_iota(jnp.int32, x.shape, 1) < HEAD_DIM
    zero = jnp.zeros_like(x)
    return jnp.concatenate([jnp.where(left, x, zero), jnp.where(left, zero, x)], axis=0)


def _dn_prep_body(y_ref, gr_ref, alog_ref, dtb_ref, w_ref, u_ref, qd_ref, qk_ref, kdt_ref, gl_ref):
    T = y_ref.shape[0]
    nch = T // CHUNK
    sls = [slice(c * CHUNK, (c + 1) * CHUNK) for c in range(nch)]
    q_of = lambda c: y_ref[sls[c], 0:HEAD_DIM]
    k_of = lambda c: y_ref[sls[c], HEAD_DIM:2 * HEAD_DIM]
    v_of = lambda c: y_ref[sls[c], 2 * HEAD_DIM:]

    rows = nch * N_GATE_ROWS
    half = lax.broadcasted_iota(jnp.int32, (N_GATE_ROWS, LANES_V7X), 1) < CHUNK
    slabs = []
    for j in range(T // LANES_V7X):
        two = gr_ref[:, j * LANES_V7X:(j + 1) * LANES_V7X]
        swapped = pltpu.roll(two, CHUNK, 1)
        slabs += [jnp.where(half, two, swapped), jnp.where(half, swapped, two)]
    gr = jnp.concatenate(slabs, axis=0)
    alog = jnp.concatenate([alog_ref[...]] * nch, axis=0)
    dtb = jnp.concatenate([dtb_ref[...]] * nch, axis=0)
    g = -jnp.exp(alog) * _softplus(gr + dtb)
    beta = jax.nn.sigmoid(gr)
    lane = lax.broadcasted_iota(jnp.int32, (rows, LANES_V7X), 1)
    l64 = lane % CHUNK
    unit = lax.broadcasted_iota(jnp.int32, (rows, LANES_V7X), 0) % N_GATE_ROWS
    pre, suf = g, g
    step = 1
    while step < CHUNK:
        pre = pre + jnp.where(l64 >= step, pltpu.roll(pre, step, 1), 0.0)
        suf = suf + jnp.where(l64 < CHUNK - step, pltpu.roll(suf, LANES_V7X - step, 1), 0.0)
        step *= 2
    cum = jnp.where(unit < 2, pre, suf)
    tot = pre + suf - g
    ecum = jnp.exp(cum)
    edec = jnp.exp(tot - cum)
    gl = jnp.exp(tot)

    cb = jnp.where(unit < 4, cum, beta)
    assert rows == LANES_V7X
    cbt = jnp.transpose(cb)
    ecbt = jnp.exp(cbt)

    def col(mat, c, r):
        idx = c * N_GATE_ROWS + r
        return jnp.broadcast_to(mat[0:CHUNK, idx:idx + 1], (CHUNK, LANES_V7X))

    def row(mat, c, r):
        idx = c * N_GATE_ROWS + r
        return mat[idx:idx + 1, :]

    ri = lax.broadcasted_iota(jnp.int32, (CHUNK, LANES_V7X), 0)
    li = lax.broadcasted_iota(jnp.int32, (CHUNK, LANES_V7X), 1)
    left = li < CHUNK
    li64 = li % CHUNK
    eye4 = jnp.concatenate([(ri == li64).astype(F32)] * 2, axis=1)

    kbs = [k_of(c).astype(BF16) for c in range(nch)]
    qkks = [_dot_nt(jnp.concatenate([q_of(c).astype(BF16), kbs[c]], axis=0),
                    jnp.concatenate([kbs[c], kbs[c]], axis=0)) for c in range(nch)]

    pms, beta4, ecum4 = [], [], []
    for c, sl in enumerate(sls):
        qc, kc = q_of(c), k_of(c)
        qk2, kk = qkks[c][0:CHUNK], qkks[c][CHUNK:]
        kt2 = jnp.transpose(jnp.concatenate([kc, kc], axis=0))
        n_pairs, beta_rows, ecum_rows = [], [], []
        for p in range(N_DIRS):
            r0, r1 = 2 * p, 2 * p + 1
            cum_row = jnp.where(left[0:1], row(cum, c, r0), row(cum, c, r1))
            cum_col = jnp.where(left, col(cbt, c, r0), col(cbt, c, r1))
            beta_col = jnp.where(left, col(cbt, c, 4 + r0), col(cbt, c, 4 + r1))
            incl = (ri >= li64) if p == 0 else (ri <= li64)
            strict = (ri > li64) if p == 0 else (ri < li64)
            dm = jnp.where(incl, jnp.exp(jnp.where(incl, cum_col - cum_row, 0.0)), 0.0)
            n_pairs.append(jnp.where(strict, kk * dm * beta_col, 0.0))
            qk_ref[p, sl, :] = (qk2 * dm).astype(BF16)
            qd_ref[p, sl, :] = jnp.concatenate(
                [qc * col(ecbt, c, r0), qc * col(ecbt, c, r1)], axis=1).astype(BF16)
            beta_rows.append(jnp.where(left[0:1], row(beta, c, 4 + r0), row(beta, c, 4 + r1)))
            ecum_rows.append(jnp.where(left[0:1], row(ecum, c, r0), row(ecum, c, r1)))
            edec_row = jnp.where(left[0:1], row(edec, c, r0), row(edec, c, r1))
            kdt_ref[p, c * 2 * CHUNK:(c + 1) * 2 * CHUNK, :] = (kt2 * edec_row).astype(BF16)
            gl_ref[p, c:c + 1, :] = jnp.concatenate([row(gl, c, r0), row(gl, c, r1)], axis=1)
        pms.append(-jnp.concatenate(n_pairs, axis=1))
        beta4.append(jnp.concatenate(beta_rows, axis=1))
        ecum4.append(jnp.concatenate(ecum_rows, axis=1))

    invs = [eye4 + pm for pm in pms]
    pbs = [pm.astype(BF16) for pm in pms]
    pws = [_dot(pb, _block_diag4(pb)) for pb in pbs]
    for _ in range(4):
        pbs = [pw.astype(BF16) for pw in pws]
        ress = [_dot(jnp.concatenate([inv.astype(BF16), pb], axis=0), _block_diag4(pb))
                for inv, pb in zip(invs, pbs)]
        invs = [inv + res[0:CHUNK] for inv, res in zip(invs, ress)]
        pws = [res[CHUNK:] for res in ress]
    invs = [inv + _dot(inv.astype(BF16), _block_diag4(pw.astype(BF16))) for inv, pw in zip(invs, pws)]

    for c, sl in enumerate(sls):
        t1 = invs[c] * beta4[c]
        t2 = t1 * ecum4[c]
        t1s = jnp.concatenate([t1[:, 0:LANES_V7X], t1[:, LANES_V7X:]], axis=0).astype(BF16)
        t2s = jnp.concatenate([t2[:, 0:LANES_V7X], t2[:, LANES_V7X:]], axis=0).astype(BF16)
        wv = _dot(t2s, _block_diag2(jnp.concatenate([kbs[c], kbs[c]], axis=1)))
        uv = _dot(t1s, _block_diag2(v_of(c).astype(BF16)))
        for p in range(N_DIRS):
            w_ref[p, sl, :] = wv[p * CHUNK:(p + 1) * CHUNK].astype(BF16)
            u_ref[p, sl, :] = uv[p * CHUNK:(p + 1) * CHUNK].astype(BF16)


def _dn_prep(y, gates_t, alog_b, dtb_b, T=PREP_TILE):
    L = y.shape[0]
    nt, nch, nc = L // T, T // CHUNK, L // CHUNK
    pair = 2 * HEAD_DIM
    out_shapes = [jax.ShapeDtypeStruct((N_DIRS, N_K_HEADS, L, pair), BF16),
                  jax.ShapeDtypeStruct((N_DIRS, N_K_HEADS, L, pair), BF16),
                  jax.ShapeDtypeStruct((N_DIRS, N_K_HEADS, L, pair), BF16),
                  jax.ShapeDtypeStruct((N_DIRS, N_K_HEADS, L, HEAD_DIM), BF16),
                  jax.ShapeDtypeStruct((N_DIRS, N_K_HEADS, nc * 2 * CHUNK, HEAD_DIM), BF16),
                  jax.ShapeDtypeStruct((N_DIRS, N_K_HEADS, nc, pair), F32)]
    big = lambda h, i: (0, h, i, 0)
    return pl.pallas_call(
        _dn_prep_body,
        grid=(N_K_HEADS, nt),
        in_specs=[pl.BlockSpec((T, KH_COLS), lambda h, i: (i, h)),
                  pl.BlockSpec((N_GATE_ROWS, T), lambda h, i: (h, i)),
                  pl.BlockSpec((None, N_GATE_ROWS, LANES_V7X), lambda h, i: (h, 0, 0)),
                  pl.BlockSpec((None, N_GATE_ROWS, LANES_V7X), lambda h, i: (h, 0, 0))],
        out_specs=[pl.BlockSpec((N_DIRS, None, T, pair), big),
                   pl.BlockSpec((N_DIRS, None, T, pair), big),
                   pl.BlockSpec((N_DIRS, None, T, pair), big),
                   pl.BlockSpec((N_DIRS, None, T, HEAD_DIM), big),
                   pl.BlockSpec((N_DIRS, None, nch * 2 * CHUNK, HEAD_DIM), big),
                   pl.BlockSpec((N_DIRS, None, nch, pair), big)],
        out_shape=out_shapes,
        compiler_params=_params(("parallel", "parallel"), T * KH_COLS * 4, 3 * N_DIRS * T * pair * 2,
                                N_DIRS * T * HEAD_DIM * 2 * 2),
        name="dn_chunk_prep",
    )(y, gates_t, alog_b, dtb_b)


def _dn_scan_body(w_ref, u_ref, qd_ref, qk_ref, kdt_ref, gl_ref, o_ref, s_ref, *, n_groups):
    grp = pl.program_id(0)
    bwd = grp >= n_groups // N_DIRS
    G, T = w_ref.shape[0], w_ref.shape[1]
    nch = T // CHUNK

    @pl.when(pl.program_id(1) == 0)
    def _():
        s_ref[...] = jnp.zeros_like(s_ref)

    states = [s_ref[g] for g in range(G)]
    for c in range(nch):
        ce = jnp.where(bwd, nch - 1 - c, c)
        r0 = pl.multiple_of(ce * CHUNK, CHUNK)
        r1 = pl.multiple_of(ce * 2 * CHUNK, 2 * CHUNK)
        rows = pl.ds(r0, CHUNK)
        res_a = [_dot(jnp.concatenate([w_ref[g, rows, :], qd_ref[g, rows, :]], axis=0),
                      _block_diag2(states[g].astype(BF16))) for g in range(G)]
        v_new = [u_ref[g, rows, :].astype(F32) - res_a[g][0:CHUNK] for g in range(G)]
        res_b = [_dot(jnp.concatenate([qk_ref[g, rows, :], kdt_ref[g, pl.ds(r1, 2 * CHUNK), :]], axis=0),
                      _block_diag2(v_new[g].astype(BF16))) for g in range(G)]
        for g in range(G):
            o_ref[rows, g * 2 * HEAD_DIM:(g + 1) * 2 * HEAD_DIM] = (res_a[g][CHUNK:] + res_b[g][0:CHUNK]).astype(BF16)
            states[g] = states[g] * gl_ref[g, pl.ds(ce, 1), :] + res_b[g][CHUNK:]
    for g in range(G):
        s_ref[g] = states[g]


def _dn_scan(w, u, qd, qk, kdt, gl, T=DN_TILE, G=SCAN_GROUP):
    nu, L, pair = w.shape
    nt, nch = L // T, T // CHUNK
    n_groups = nu // G
    per_dir = n_groups // N_DIRS

    def tile(gi, i):
        return jnp.where(gi >= per_dir, nt - 1 - i, i)

    big = lambda gi, i: (gi, tile(gi, i), 0)
    return pl.pallas_call(
        functools.partial(_dn_scan_body, n_groups=n_groups),
        grid=(n_groups, nt),
        in_specs=[pl.BlockSpec((G, T, pair), big),
                  pl.BlockSpec((G, T, pair), big),
                  pl.BlockSpec((G, T, pair), big),
                  pl.BlockSpec((G, T, HEAD_DIM), big),
                  pl.BlockSpec((G, nch * 2 * CHUNK, HEAD_DIM), big),
                  pl.BlockSpec((G, nch, pair), big)],
        out_specs=pl.BlockSpec((None, T, G * pair), lambda gi, i: (gi // per_dir, tile(gi, i), gi % per_dir)),
        out_shape=jax.ShapeDtypeStruct((N_DIRS, L, VALUE_DIM), BF16),
        scratch_shapes=[pltpu.VMEM((G, HEAD_DIM, pair), F32)],
        compiler_params=_params(("parallel", "arbitrary"), 3 * G * T * pair * 2, 2 * G * T * HEAD_DIM * 2,
                                T * G * pair * 2),
        name="dn_state_scan",
    )(w, u, qd, qk, kdt, gl)


def _dn_out_body(of_ref, ob_ref, z_ref, nw_ref, wout_ref, res_ref, out_ref):
    nw = nw_ref[...]
    pieces = []
    for h in range(N_V_HEADS):
        sl = slice(h * HEAD_DIM, (h + 1) * HEAD_DIM)
        o = of_ref[:, sl].astype(F32) + ob_ref[:, sl].astype(F32)
        y = _rms(o, nw) * _silu(z_ref[:, sl].astype(F32))
        pieces.append(y.astype(BF16))
    out_ref[...] = res_ref[...] + _dot(jnp.concatenate(pieces, axis=1), wout_ref[...])


def _dn_out(o, z, nw, wout, res, tm=512):
    L, D = res.shape
    V = z.shape[1]
    return pl.pallas_call(
        _dn_out_body,
        grid=(L // tm,),
        in_specs=[pl.BlockSpec((None, tm, V), lambda i: (0, i, 0)),
                  pl.BlockSpec((None, tm, V), lambda i: (1, i, 0)),
                  pl.BlockSpec((tm, V), lambda i: (i, 0)),
                  pl.BlockSpec((1, HEAD_DIM), lambda i: (0, 0)),
                  pl.BlockSpec((V, D), lambda i: (0, 0)),
                  pl.BlockSpec((tm, D), lambda i: (i, 0))],
        out_specs=pl.BlockSpec((tm, D), lambda i: (i, 0)),
        out_shape=jax.ShapeDtypeStruct((L, D), F32),
        compiler_params=_params(("parallel",), 3 * tm * V * 2, V * D * 2, 2 * tm * D * 4),
        name="dn_out_proj",
    )(o, o, z, nw, wout, res)


def _ffn_up_body(x_ref, nw_ref, wg_ref, wu_ref, a_ref):
    h = _rms(x_ref[...], nw_ref[...]).astype(BF16)
    a_ref[...] = (_silu(_dot(h, wg_ref[...])) * _dot(h, wu_ref[...])).astype(BF16)


def _ffn_up(x, nw, wgu, layer, tm=1024, nj=2):
    L, D = x.shape
    dff = wgu.shape[2] // 2
    tn = dff // nj
    return pl.pallas_call(
        _ffn_up_body,
        grid=(L // tm, nj),
        in_specs=[pl.BlockSpec((tm, D), lambda i, j: (i, 0)),
                  pl.BlockSpec((1, D), lambda i, j: (0, 0)),
                  pl.BlockSpec((None, D, tn), lambda i, j: (layer, 0, j)),
                  pl.BlockSpec((None, D, tn), lambda i, j: (layer, 0, nj + j))],
        out_specs=pl.BlockSpec((tm, tn), lambda i, j: (i, j)),
        out_shape=jax.ShapeDtypeStruct((L, dff), BF16),
        compiler_params=_params(("parallel", "arbitrary"), tm * D * 4, 2 * D * tn * 2, tm * tn * 2,
                                tm * tn * 4),
        name="ffn_gate_up",
    )(x, nw, wgu, wgu)


def _ffn_down_body(a_ref, w_ref, res_ref, nw_ref, o_ref, *, final_norm):
    y = res_ref[...] + _dot(a_ref[...], w_ref[...])
    o_ref[...] = _rms(y, nw_ref[...]) if final_norm else y


def _ffn_down(a, w, layer, res, nw, final_norm, tm=512):
    L, D = res.shape
    dff = a.shape[1]
    return pl.pallas_call(
        functools.partial(_ffn_down_body, final_norm=final_norm),
        grid=(L // tm,),
        in_specs=[pl.BlockSpec((tm, dff), lambda i: (i, 0)),
                  pl.BlockSpec((None, dff, D), lambda i: (layer, 0, 0)),
                  pl.BlockSpec((tm, D), lambda i: (i, 0)),
                  pl.BlockSpec((1, D), lambda i: (0, 0))],
        out_specs=pl.BlockSpec((tm, D), lambda i: (i, 0)),
        out_shape=jax.ShapeDtypeStruct((L, D), F32),
        compiler_params=_params(("parallel",), tm * dff * 2, dff * D * 2, 2 * tm * D * 4),
        name="ffn_down",
    )(a, w, res, nw)


def _fourier1_body(x_ref, nw_ref, tab_ref, y_ref):
    nb = tab_ref.shape[0]
    n1 = x_ref.shape[0]
    D = nw_ref.shape[1]
    for j in range(nb):
        h = _rms(x_ref[:, j * D:(j + 1) * D], nw_ref[...]).astype(BF16)
        y = _dot(tab_ref[j], h)
        y_ref[0, j] = y[0:n1].astype(BF16)
        y_ref[1, j] = y[n1:].astype(BF16)


def _fourier2_body(f_ref, y_ref, z_ref):
    z_ref[...] = _dot(f_ref[...], y_ref[...]).astype(BF16)


def _fourier_out_body(pr_ref, pi_ref, cc_ref, sc_ref, w_ref, b_ref, res_ref, o_ref):
    pieces = []
    for g in range(N_FOURIER_GROUPS):
        sl = slice(g * GROUP_DIM, (g + 1) * GROUP_DIM)
        m = _dot(pr_ref[:, sl], cc_ref[...]) + _dot(pi_ref[:, sl], sc_ref[...])
        pieces.append(m.astype(BF16))
    o_ref[...] = res_ref[...] + (_dot(jnp.concatenate(pieces, axis=1), w_ref[...]) + b_ref[...])


def _dft_tables(L, n1, n2):
    two_pi = 2.0 * math.pi
    i2 = lax.broadcasted_iota(jnp.int32, (n2, n1, n1), 0)
    k1 = lax.broadcasted_iota(jnp.int32, (n2, n1, n1), 1)
    i1 = lax.broadcasted_iota(jnp.int32, (n2, n1, n1), 2)
    ang = ((k1 * (i2 + n2 * i1)) % L).astype(F32) * (two_pi / L)
    s1 = 1.0 / math.sqrt(n1)
    tab1 = jnp.concatenate([jnp.cos(ang) * s1, -jnp.sin(ang) * s1], axis=1).astype(BF16)
    a = lax.broadcasted_iota(jnp.int32, (n2, n2), 0)
    b = lax.broadcasted_iota(jnp.int32, (n2, n2), 1)
    ang2 = ((a * b) % n2).astype(F32) * (two_pi / n2)
    s2 = 1.0 / math.sqrt(n2)
    c2, sn2 = jnp.cos(ang2) * s2, jnp.sin(ang2) * s2
    tab2 = jnp.concatenate([jnp.concatenate([c2, sn2], axis=1),
                            jnp.concatenate([-sn2, c2], axis=1)], axis=0).astype(BF16)
    a = lax.broadcasted_iota(jnp.int32, (GROUP_DIM, GROUP_DIM), 0)
    b = lax.broadcasted_iota(jnp.int32, (GROUP_DIM, GROUP_DIM), 1)
    angc = ((a * b) % GROUP_DIM).astype(F32) * (two_pi / GROUP_DIM)
    sc = 1.0 / math.sqrt(GROUP_DIM)
    return tab1, tab2, (jnp.cos(angc) * sc).astype(BF16), (jnp.sin(angc) * sc).astype(BF16)


def _fourier_mixer(x, nw, wout, bout, nb=4, tm=512):
    L, D = x.shape
    n1 = n2 = math.isqrt(L)
    assert n1 * n2 == L
    tab1, tab2, cc, sc = _dft_tables(L, n1, n2)
    y = pl.pallas_call(
        _fourier1_body,
        grid=(n2 // nb,),
        in_specs=[pl.BlockSpec((n1, nb * D), lambda i: (0, i)),
                  pl.BlockSpec((1, D), lambda i: (0, 0)),
                  pl.BlockSpec((nb, 2 * n1, n1), lambda i: (i, 0, 0))],
        out_specs=pl.BlockSpec((2, nb, n1, D), lambda i: (0, i, 0, 0)),
        out_shape=jax.ShapeDtypeStruct((2, n2, n1, D), BF16),
        compiler_params=_params(("parallel",), n1 * nb * D * 4, nb * 2 * n1 * n1 * 2, 2 * nb * n1 * D * 2),
        name="fourier_stage1",
    )(x.reshape(n1, n2 * D), nw, tab1)
    tn = nb * D
    z = pl.pallas_call(
        _fourier2_body,
        grid=(n1 * D // tn,),
        in_specs=[pl.BlockSpec((2 * n2, 2 * n2), lambda i: (0, 0)),
                  pl.BlockSpec((2 * n2, tn), lambda i: (0, i))],
        out_specs=pl.BlockSpec((2 * n2, tn), lambda i: (0, i)),
        out_shape=jax.ShapeDtypeStruct((2 * n2, n1 * D), BF16),
        compiler_params=_params(("parallel",), 2 * 2 * n2 * tn * 2, 2 * n2 * tn * 4),
        name="fourier_stage2",
    )(tab2, y.reshape(2 * n2, n1 * D))
    p = z.reshape(2, L, D)
    return pl.pallas_call(
        _fourier_out_body,
        grid=(L // tm,),
        in_specs=[pl.BlockSpec((None, tm, D), lambda i: (0, i, 0)),
                  pl.BlockSpec((None, tm, D), lambda i: (1, i, 0)),
                  pl.BlockSpec((GROUP_DIM, GROUP_DIM), lambda i: (0, 0)),
                  pl.BlockSpec((GROUP_DIM, GROUP_DIM), lambda i: (0, 0)),
                  pl.BlockSpec((D, D), lambda i: (0, 0)),
                  pl.BlockSpec((1, D), lambda i: (0, 0)),
                  pl.BlockSpec((tm, D), lambda i: (i, 0))],
        out_specs=pl.BlockSpec((tm, D), lambda i: (i, 0)),
        out_shape=jax.ShapeDtypeStruct((L, D), F32),
        compiler_params=_params(("parallel",), 2 * tm * D * 2, D * D * 2, 2 * tm * D * 4),
        name="fourier_out_proj",
    )(p, p, cc, sc, wout, bout, x)


def _kh_columns(a):
    lead = a.shape[:-1]
    q = a[..., :KEY_DIM].reshape(*lead, N_K_HEADS, HEAD_DIM)
    k = a[..., KEY_DIM:2 * KEY_DIM].reshape(*lead, N_K_HEADS, HEAD_DIM)
    v = a[..., 2 * KEY_DIM:].reshape(*lead, N_K_HEADS, 2 * HEAD_DIM)
    return jnp.concatenate([q, k, v], axis=-1).reshape(*lead, CONV_DIM)


def _gate_column_order():
    cols = []
    for h in range(N_K_HEADS):
        units = [d * N_V_HEADS + 2 * h + s for d in range(N_DIRS) for s in range(2)]
        cols += [N_DIRS * N_V_HEADS + un for un in units] + units
    return jnp.asarray(cols, jnp.int32)


def _unit_rows(p):
    per = p.reshape(N_DIRS, N_K_HEADS, 2).transpose(1, 0, 2).reshape(N_K_HEADS, 2 * N_DIRS)
    per = jnp.concatenate([per, jnp.zeros_like(per)], axis=1)
    return jnp.broadcast_to(per[:, :, None], (N_K_HEADS, N_GATE_ROWS, LANES_V7X)).astype(F32)


def _deltanet_layer(x, nw, w_in, conv_w, a_log, dt_bias, out_norm_w, w_out):
    L = x.shape[0]
    nc = L // CHUNK
    gate0 = CONV_DIM + VALUE_DIM
    wqkv = _kh_columns(w_in[:, :CONV_DIM]).astype(BF16)
    wz = w_in[:, CONV_DIM:gate0].astype(BF16)
    n_gate = 2 * N_DIRS * N_V_HEADS
    wg = jnp.pad(w_in[:, gate0:][:, _gate_column_order()].T, ((0, LANES_V7X - n_gate), (0, 0))).astype(BF16)
    cw = jnp.pad(_kh_columns(conv_w), ((0, SUBLANES_V7X - CONV_WIDTH), (0, 0)))

    y, z, gates = _in_proj(x, nw, wqkv, cw, wz, wg)
    w, u, qd, qk, kdt, gl = _dn_prep(y, gates, _unit_rows(a_log), _unit_rows(dt_bias))
    nu = N_DIRS * N_K_HEADS
    o = _dn_scan(w.reshape(nu, L, -1), u.reshape(nu, L, -1), qd.reshape(nu, L, -1), qk.reshape(nu, L, -1),
                 kdt.reshape(nu, nc * 2 * CHUNK, -1), gl.reshape(nu, nc, -1))
    return _dn_out(o, z, out_norm_w.reshape(1, -1), w_out.astype(BF16), x)


def kernel(x, mix_norm_w, ffn_norm_w, dn_w_in, dn_conv_w, dn_a_log, dn_dt_bias, dn_out_norm_w, dn_w_out,
           fn_w_out, fn_b_out, ffn_w_gate_up, ffn_w_down, final_norm_w):
    B, L, D = x.shape
    row = lambda v: v.reshape(1, -1)
    wgu, wdn = ffn_w_gate_up.astype(BF16), ffn_w_down.astype(BF16)
    outs = []
    for b in range(B):
        h = x[b]
        h = _deltanet_layer(h, row(mix_norm_w[0]), dn_w_in[0], dn_conv_w[0], dn_a_log[0], dn_dt_bias[0],
                            dn_out_norm_w[0], dn_w_out[0])
        a = _ffn_up(h, row(ffn_norm_w[0]), wgu, 0)
        h = _ffn_down(a, wdn, 0, h, row(final_norm_w), final_norm=False)
        h = _fourier_mixer(h, row(mix_norm_w[1]), fn_w_out[0].astype(BF16), row(fn_b_out[0]))
        a = _ffn_up(h, row(ffn_norm_w[1]), wgu, 1)
        h = _ffn_down(a, wdn, 1, h, row(final_norm_w), final_norm=True)
        outs.append(h)
    return jnp.stack(outs, axis=0)
```

```python
import functools
import math

import jax
import jax.numpy as jnp
from jax import lax
from jax.experimental import pallas as pl
from jax.experimental.pallas import tpu as pltpu

F32 = jnp.float32
BF16 = jnp.bfloat16

LANES_V7X = 128
SUBLANES_V7X = 8
VMEM_BYTES_V7X = 64 * 1024 * 1024
VMEM_LIMIT_CAP = VMEM_BYTES_V7X - 8 * 1024 * 1024

D_MODEL = 1024
N_K_HEADS = 8
N_V_HEADS = 16
HEAD_DIM = 128
KEY_DIM = N_K_HEADS * HEAD_DIM
VALUE_DIM = N_V_HEADS * HEAD_DIM
CONV_DIM = 2 * KEY_DIM + VALUE_DIM
CONV_WIDTH = 5
CONV_HALO = 16
CHUNK = 64
N_DIRS = 2
KH_COLS = 2 * HEAD_DIM + 2 * HEAD_DIM
N_GATE_ROWS = 8
N_FOURIER_GROUPS = 4
GROUP_DIM = D_MODEL // N_FOURIER_GROUPS
D_FF = 2816
RMS_EPS = 1e-6
L2_EPS = 1e-6

PREP_TILE = 2048
DN_TILE = 512
SCAN_GROUP = 8


def _params(semantics, *block_bytes):
    need = 2 * sum(block_bytes) + 24 * 1024 * 1024
    return pltpu.CompilerParams(dimension_semantics=semantics,
                                vmem_limit_bytes=int(min(need, VMEM_LIMIT_CAP)))


def _rms(x, w):
    return x * lax.rsqrt(jnp.mean(x * x, axis=-1, keepdims=True) + RMS_EPS) * w


def _silu(x):
    return x * jax.nn.sigmoid(x)


def _softplus(x):
    return jnp.maximum(x, 0.0) + jnp.log1p(jnp.exp(-jnp.abs(x)))


def _dot(a, b):
    return jnp.dot(a, b, preferred_element_type=F32)


def _dot_nt(a, b):
    return lax.dot_general(a, b, (((1,), (1,)), ((), ())), preferred_element_type=F32)


def _inproj_body(xm_ref, xp_ref, xn_ref, nw_ref, wqkv_ref, cw_ref, wz_ref, wg_ref,
                 qkv_ref, z_ref, g_ref, h_ref, p_ref):
    i, j = pl.program_id(0), pl.program_id(1)
    tm = xm_ref.shape[0]

    @pl.when(j == 0)
    def _():
        nw = nw_ref[...]
        hm = _rms(xm_ref[...], nw).astype(BF16)
        h_ref[0:CONV_HALO, :] = jnp.where(i > 0, _rms(xp_ref[...], nw), 0.0).astype(BF16)
        h_ref[CONV_HALO:CONV_HALO + tm, :] = hm
        h_ref[CONV_HALO + tm:, :] = jnp.where(i < pl.num_programs(0) - 1, _rms(xn_ref[...], nw), 0.0).astype(BF16)
        g_ref[...] = _dot_nt(wg_ref[...], hm)

    proj = _dot(h_ref[...], wqkv_ref[...])
    n_slabs = p_ref.shape[0]
    for c in range(n_slabs):
        p_ref[c] = proj[:, c * LANES_V7X:(c + 1) * LANES_V7X]
    first = CONV_HALO - (CONV_WIDTH - 1) // 2
    slabs_per_head = KH_COLS // LANES_V7X
    for c in range(n_slabs):
        lanes = slice(c * LANES_V7X, (c + 1) * LANES_V7X)
        y = p_ref[c, pl.ds(first, tm), :] * cw_ref[0:1, lanes]
        for tap in range(1, CONV_WIDTH):
            y = y + p_ref[c, pl.ds(first + tap, tm), :] * cw_ref[tap:tap + 1, lanes]
        y = _silu(y)
        role = c % slabs_per_head
        if role < 2:
            y = y * lax.rsqrt(jnp.sum(y * y, axis=-1, keepdims=True) + L2_EPS)
        if role == 0:
            y = y * (HEAD_DIM ** -0.5)
        qkv_ref[:, lanes] = y
    z_ref[...] = _dot(h_ref[CONV_HALO:CONV_HALO + tm, :], wz_ref[...]).astype(BF16)


def _in_proj(x, nw, wqkv, cw, wz, wg, tm=1024, nj=4):
    L, D = x.shape
    tq, tz = wqkv.shape[1] // nj, wz.shape[1] // nj
    hb = tm // CONV_HALO
    last_halo = L // CONV_HALO - 1
    return pl.pallas_call(
        _inproj_body,
        grid=(L // tm, nj),
        in_specs=[pl.BlockSpec((tm, D), lambda i, j: (i, 0)),
                  pl.BlockSpec((CONV_HALO, D), lambda i, j: (jnp.maximum(i * hb - 1, 0), 0)),
                  pl.BlockSpec((CONV_HALO, D), lambda i, j: (jnp.minimum((i + 1) * hb, last_halo), 0)),
                  pl.BlockSpec((1, D), lambda i, j: (0, 0)),
                  pl.BlockSpec((D, tq), lambda i, j: (0, j)),
                  pl.BlockSpec((SUBLANES_V7X, tq), lambda i, j: (0, j)),
                  pl.BlockSpec((D, tz), lambda i, j: (0, j)),
                  pl.BlockSpec((LANES_V7X, D), lambda i, j: (0, 0))],
        out_specs=[pl.BlockSpec((tm, tq), lambda i, j: (i, j)),
                   pl.BlockSpec((tm, tz), lambda i, j: (i, j)),
                   pl.BlockSpec((LANES_V7X, tm), lambda i, j: (0, i))],
        out_shape=[jax.ShapeDtypeStruct((L, wqkv.shape[1]), F32),
                   jax.ShapeDtypeStruct((L, wz.shape[1]), BF16),
                   jax.ShapeDtypeStruct((LANES_V7X, L), F32)],
        scratch_shapes=[pltpu.VMEM((tm + 2 * CONV_HALO, D), BF16),
                        pltpu.VMEM((tq // LANES_V7X, tm + 2 * CONV_HALO, LANES_V7X), F32)],
        compiler_params=_params(("parallel", "arbitrary"), tm * D * 4, D * tq * 2, D * tz * 2,
                                tm * tq * 4, tm * tz * 2, tm * LANES_V7X * 4, tm * D, tm * tq * 2),
        name="dn_in_proj",
    )(x, x, x, nw, wqkv, cw, wz, wg)


def _block_diag4(x):
    blk = lax.broadcasted_iota(jnp.int32, x.shape, 1) // CHUNK
    zero = jnp.zeros_like(x)
    return jnp.concatenate([jnp.where(blk == r, x, zero) for r in range(4)], axis=0)


def _block_diag2(x):
    left = lax.broadcasted_iota(jnp.int32, x.shape, 1) < HEAD_DIM
    zero = jnp.zeros_like(x)
    return jnp.concatenate([jnp.where(left, x, zero), jnp.where(left, zero, x)], axis=0)


def _dn_prep_body(y_ref, gr_ref, alog_ref, dtb_ref, w_ref, u_ref, qd_ref, qk_ref, kdt_ref, gl_ref):
    T = y_ref.shape[0]
    nch = T // CHUNK
    sls = [slice(c * CHUNK, (c + 1) * CHUNK) for c in range(nch)]
    q_of = lambda c: y_ref[sls[c], 0:HEAD_DIM]
    k_of = lambda c: y_ref[sls[c], HEAD_DIM:2 * HEAD_DIM]
    v_of = lambda c: y_ref[sls[c], 2 * HEAD_DIM:]

    rows = nch * N_GATE_ROWS
    half = lax.broadcasted_iota(jnp.int32, (N_GATE_ROWS, LANES_V7X), 1) < CHUNK
    slabs = []
    for j in range(T // LANES_V7X):
        two = gr_ref[:, j * LANES_V7X:(j + 1) * LANES_V7X]
        swapped = pltpu.roll(two, CHUNK, 1)
        slabs += [jnp.where(half, two, swapped), jnp.where(half, swapped, two)]
    gr = jnp.concatenate(slabs, axis=0)
    alog = jnp.concatenate([alog_ref[...]] * nch, axis=0)
    dtb = jnp.concatenate([dtb_ref[...]] * nch, axis=0)
    g = -jnp.exp(alog) * _softplus(gr + dtb)
    beta = jax.nn.sigmoid(gr)
    lane = lax.broadcasted_iota(jnp.int32, (rows, LANES_V7X), 1)
    l64 = lane % CHUNK
    unit = lax.broadcasted_iota(jnp.int32, (rows, LANES_V7X), 0) % N_GATE_ROWS
    pre, suf = g, g
    step = 1
    while step < CHUNK:
        pre = pre + jnp.where(l64 >= step, pltpu.roll(pre, step, 1), 0.0)
        suf = suf + jnp.where(l64 < CHUNK - step, pltpu.roll(suf, LANES_V7X - step, 1), 0.0)
        step *= 2
    cum = jnp.where(unit < 2, pre, suf)
    tot = pre + suf - g
    ecum = jnp.exp(cum)
    edec = jnp.exp(tot - cum)
    gl = jnp.exp(tot)

    cb = jnp.where(unit < 4, cum, beta)
    assert rows % LANES_V7X == 0
    cbt = [jnp.transpose(cb[b:b + LANES_V7X]) for b in range(0, rows, LANES_V7X)]

    def col(mats, c, r):
        blk, idx = divmod(c * N_GATE_ROWS + r, LANES_V7X)
        return jnp.broadcast_to(mats[blk][0:CHUNK, idx:idx + 1], (CHUNK, LANES_V7X))

    def row(mat, c, r):
        idx = c * N_GATE_ROWS + r
        return mat[idx:idx + 1, :]

    ri = lax.broadcasted_iota(jnp.int32, (CHUNK, LANES_V7X), 0)
    li = lax.broadcasted_iota(jnp.int32, (CHUNK, LANES_V7X), 1)
    left = li < CHUNK
    li64 = li % CHUNK
    eye4 = jnp.concatenate([(ri == li64).astype(F32)] * 2, axis=1)

    def qkk_of(c):
        kb = k_of(c).astype(BF16)
        return _dot_nt(jnp.concatenate([q_of(c).astype(BF16), kb], axis=0),
                       jnp.concatenate([kb, kb], axis=0))

    def decay_of(c, qkk):
        sl = sls[c]
        qc, kc = q_of(c), k_of(c)
        qk2, kk = qkk[0:CHUNK], qkk[CHUNK:]
        kt2 = jnp.transpose(jnp.concatenate([kc, kc], axis=0))
        n_pairs, beta_rows, ecum_rows = [], [], []
        for p in range(N_DIRS):
            r0, r1 = 2 * p, 2 * p + 1
            cum_row = jnp.where(left[0:1], row(cum, c, r0), row(cum, c, r1))
            col0, col1 = col(cbt, c, r0), col(cbt, c, r1)
            cum_col = jnp.where(left, col0, col1)
            beta_col = jnp.where(left, col(cbt, c, 4 + r0), col(cbt, c, 4 + r1))
            incl = (ri >= li64) if p == 0 else (ri <= li64)
            strict = (ri > li64) if p == 0 else (ri < li64)
            dm = jnp.where(incl, jnp.exp(jnp.where(incl, cum_col - cum_row, 0.0)), 0.0)
            n_pairs.append(jnp.where(strict, kk * dm * beta_col, 0.0))
            qk_ref[p, sl, :] = (qk2 * dm).astype(BF16)
            qd_ref[p, sl, :] = jnp.concatenate([qc * jnp.exp(col0), qc * jnp.exp(col1)], axis=1).astype(BF16)
            beta_rows.append(jnp.where(left[0:1], row(beta, c, 4 + r0), row(beta, c, 4 + r1)))
            ecum_rows.append(jnp.where(left[0:1], row(ecum, c, r0), row(ecum, c, r1)))
            edec_row = jnp.where(left[0:1], row(edec, c, r0), row(edec, c, r1))
            kdt_ref[p, c * 2 * CHUNK:(c + 1) * 2 * CHUNK, :] = (kt2 * edec_row).astype(BF16)
            gl_ref[p, c:c + 1, :] = jnp.concatenate([row(gl, c, r0), row(gl, c, r1)], axis=1)
        return (-jnp.concatenate(n_pairs, axis=1), jnp.concatenate(beta_rows, axis=1),
                jnp.concatenate(ecum_rows, axis=1))

    def solve_of(c, inv, beta_row, ecum_row):
        t1 = inv * beta_row
        t2 = t1 * ecum_row
        t1s = jnp.concatenate([t1[:, 0:LANES_V7X], t1[:, LANES_V7X:]], axis=0).astype(BF16)
        t2s = jnp.concatenate([t2[:, 0:LANES_V7X], t2[:, LANES_V7X:]], axis=0).astype(BF16)
        kb = k_of(c).astype(BF16)
        wv = _dot(t2s, _block_diag2(jnp.concatenate([kb, kb], axis=1)))
        uv = _dot(t1s, _block_diag2(v_of(c).astype(BF16)))
        for p in range(N_DIRS):
            w_ref[p, sls[c], :] = wv[p * CHUNK:(p + 1) * CHUNK].astype(BF16)
            u_ref[p, sls[c], :] = uv[p * CHUNK:(p + 1) * CHUNK].astype(BF16)

    def inverse_phases(pms):
        invs = [eye4 + pm for pm in pms]
        pbs = [pm.astype(BF16) for pm in pms]
        pws = [_dot(pb, _block_diag4(pb)) for pb in pbs]
        yield None
        for _ in range(4):
            pbs = [pw.astype(BF16) for pw in pws]
            ress = [_dot(jnp.concatenate([inv.astype(BF16), pb], axis=0), _block_diag4(pb))
                    for inv, pb in zip(invs, pbs)]
            invs = [inv + res[0:CHUNK] for inv, res in zip(invs, ress)]
            pws = [res[CHUNK:] for res in ress]
            yield None
        yield [inv + _dot(inv.astype(BF16), _block_diag4(pw.astype(BF16))) for inv, pw in zip(invs, pws)]

    def spread(items, n):
        return [items[len(items) * i // n:len(items) * (i + 1) // n] for i in range(n)]

    n_phases = 6
    half_a, half_b = list(range(nch // 2)), list(range(nch // 2, nch))
    dec = {c: decay_of(c, qkk_of(c)) for c in half_a}
    qkk_b = {c: qkk_of(c) for c in half_b}
    inv_a = None
    for piece, out in zip(spread(half_b, n_phases), inverse_phases([dec[c][0] for c in half_a])):
        inv_a = out
        for c in piece:
            dec[c] = decay_of(c, qkk_b[c])
    inv_b = None
    for piece, out in zip(spread(half_a, n_phases), inverse_phases([dec[c][0] for c in half_b])):
        inv_b = out
        for i in piece:
            solve_of(half_a[i], inv_a[i], dec[half_a[i]][1], dec[half_a[i]][2])
    for i, c in enumerate(half_b):
        solve_of(c, inv_b[i], dec[c][1], dec[c][2])


def _dn_prep(y, gates_t, alog_b, dtb_b, T=PREP_TILE):
    L = y.shape[0]
    nt, nch, nc = L // T, T // CHUNK, L // CHUNK
    pair = 2 * HEAD_DIM
    out_shapes = [jax.ShapeDtypeStruct((N_DIRS, N_K_HEADS, L, pair), BF16),
                  jax.ShapeDtypeStruct((N_DIRS, N_K_HEADS, L, pair), BF16),
                  jax.ShapeDtypeStruct((N_DIRS, N_K_HEADS, L, pair), BF16),
                  jax.ShapeDtypeStruct((N_DIRS, N_K_HEADS, L, HEAD_DIM), BF16),
                  jax.ShapeDtypeStruct((N_DIRS, N_K_HEADS, nc * 2 * CHUNK, HEAD_DIM), BF16),
                  jax.ShapeDtypeStruct((N_DIRS, N_K_HEADS, nc, pair), F32)]
    big = lambda h, i: (0, h, i, 0)
    return pl.pallas_call(
        _dn_prep_body,
        grid=(N_K_HEADS, nt),
        in_specs=[pl.BlockSpec((T, KH_COLS), lambda h, i: (i, h)),
                  pl.BlockSpec((N_GATE_ROWS, T), lambda h, i: (h, i)),
                  pl.BlockSpec((None, N_GATE_ROWS, LANES_V7X), lambda h, i: (h, 0, 0)),
                  pl.BlockSpec((None, N_GATE_ROWS, LANES_V7X), lambda h, i: (h, 0, 0))],
        out_specs=[pl.BlockSpec((N_DIRS, None, T, pair), big),
                   pl.BlockSpec((N_DIRS, None, T, pair), big),
                   pl.BlockSpec((N_DIRS, None, T, pair), big),
                   pl.BlockSpec((N_DIRS, None, T, HEAD_DIM), big),
                   pl.BlockSpec((N_DIRS, None, nch * 2 * CHUNK, HEAD_DIM), big),
                   pl.BlockSpec((N_DIRS, None, nch, pair), big)],
        out_shape=out_shapes,
        compiler_params=_params(("parallel", "parallel"), T * KH_COLS * 4, 3 * N_DIRS * T * pair * 2,
                                N_DIRS * T * HEAD_DIM * 2 * 2),
        name="dn_chunk_prep",
    )(y, gates_t, alog_b, dtb_b)


def _dn_scan_body(w_ref, u_ref, qd_ref, qk_ref, kdt_ref, gl_ref, o_ref, s_ref, *, n_groups):
    grp = pl.program_id(0)
    bwd = grp >= n_groups // N_DIRS
    G, T = w_ref.shape[0], w_ref.shape[1]
    nch = T // CHUNK

    @pl.when(pl.program_id(1) == 0)
    def _():
        s_ref[...] = jnp.zeros_like(s_ref)

    states = [s_ref[g] for g in range(G)]
    for c in range(nch):
        ce = jnp.where(bwd, nch - 1 - c, c)
        r0 = pl.multiple_of(ce * CHUNK, CHUNK)
        r1 = pl.multiple_of(ce * 2 * CHUNK, 2 * CHUNK)
        rows = pl.ds(r0, CHUNK)
        res_a = [_dot(jnp.concatenate([w_ref[g, rows, :], qd_ref[g, rows, :]], axis=0),
                      _block_diag2(states[g].astype(BF16))) for g in range(G)]
        v_new = [u_ref[g, rows, :].astype(F32) - res_a[g][0:CHUNK] for g in range(G)]
        res_b = [_dot(jnp.concatenate([qk_ref[g, rows, :], kdt_ref[g, pl.ds(r1, 2 * CHUNK), :]], axis=0),
                      _block_diag2(v_new[g].astype(BF16))) for g in range(G)]
        for g in range(G):
            o_ref[rows, g * 2 * HEAD_DIM:(g + 1) * 2 * HEAD_DIM] = (res_a[g][CHUNK:] + res_b[g][0:CHUNK]).astype(BF16)
            states[g] = states[g] * gl_ref[g, pl.ds(ce, 1), :] + res_b[g][CHUNK:]
    for g in range(G):
        s_ref[g] = states[g]


def _dn_scan(w, u, qd, qk, kdt, gl, T=DN_TILE, G=SCAN_GROUP):
    nu, L, pair = w.shape
    nt, nch = L // T, T // CHUNK
    n_groups = nu // G
    per_dir = n_groups // N_DIRS

    def tile(gi, i):
        return jnp.where(gi >= per_dir, nt - 1 - i, i)

    big = lambda gi, i: (gi, tile(gi, i), 0)
    return pl.pallas_call(
        functools.partial(_dn_scan_body, n_groups=n_groups),
        grid=(n_groups, nt),
        in_specs=[pl.BlockSpec((G, T, pair), big),
                  pl.BlockSpec((G, T, pair), big),
                  pl.BlockSpec((G, T, pair), big),
                  pl.BlockSpec((G, T, HEAD_DIM), big),
                  pl.BlockSpec((G, nch * 2 * CHUNK, HEAD_DIM), big),
                  pl.BlockSpec((G, nch, pair), big)],
        out_specs=pl.BlockSpec((None, T, G * pair), lambda gi, i: (gi // per_dir, tile(gi, i), gi % per_dir)),
        out_shape=jax.ShapeDtypeStruct((N_DIRS, L, VALUE_DIM), BF16),
        scratch_shapes=[pltpu.VMEM((G, HEAD_DIM, pair), F32)],
        compiler_params=_params(("parallel", "arbitrary"), 3 * G * T * pair * 2, 2 * G * T * HEAD_DIM * 2,
                                T * G * pair * 2),
        name="dn_state_scan",
    )(w, u, qd, qk, kdt, gl)


def _dn_out_body(of_ref, ob_ref, z_ref, nw_ref, wout_ref, res_ref, out_ref):
    nw = nw_ref[...]
    pieces = []
    for h in range(N_V_HEADS):
        sl = slice(h * HEAD_DIM, (h + 1) * HEAD_DIM)
        o = of_ref[:, sl].astype(F32) + ob_ref[:, sl].astype(F32)
        y = _rms(o, nw) * _silu(z_ref[:, sl].astype(F32))
        pieces.append(y.astype(BF16))
    out_ref[...] = res_ref[...] + _dot(jnp.concatenate(pieces, axis=1), wout_ref[...])


def _dn_out(o, z, nw, wout, res, tm=512):
    L, D = res.shape
    V = z.shape[1]
    return pl.pallas_call(
        _dn_out_body,
        grid=(L // tm,),
        in_specs=[pl.BlockSpec((None, tm, V), lambda i: (0, i, 0)),
                  pl.BlockSpec((None, tm, V), lambda i: (1, i, 0)),
                  pl.BlockSpec((tm, V), lambda i: (i, 0)),
                  pl.BlockSpec((1, HEAD_DIM), lambda i: (0, 0)),
                  pl.BlockSpec((V, D), lambda i: (0, 0)),
                  pl.BlockSpec((tm, D), lambda i: (i, 0))],
        out_specs=pl.BlockSpec((tm, D), lambda i: (i, 0)),
        out_shape=jax.ShapeDtypeStruct((L, D), F32),
        compiler_params=_params(("parallel",), 3 * tm * V * 2, V * D * 2, 2 * tm * D * 4),
        name="dn_out_proj",
    )(o, o, z, nw, wout, res)


def _ffn_up_body(x_ref, nw_ref, wg_ref, wu_ref, a_ref):
    h = _rms(x_ref[...], nw_ref[...]).astype(BF16)
    a_ref[...] = (_silu(_dot(h, wg_ref[...])) * _dot(h, wu_ref[...])).astype(BF16)


def _ffn_up(x, nw, wgu, layer, tm=1024, nj=2):
    L, D = x.shape
    dff = wgu.shape[2] // 2
    tn = dff // nj
    return pl.pallas_call(
        _ffn_up_body,
        grid=(L // tm, nj),
        in_specs=[pl.BlockSpec((tm, D), lambda i, j: (i, 0)),
                  pl.BlockSpec((1, D), lambda i, j: (0, 0)),
                  pl.BlockSpec((None, D, tn), lambda i, j: (layer, 0, j)),
                  pl.BlockSpec((None, D, tn), lambda i, j: (layer, 0, nj + j))],
        out_specs=pl.BlockSpec((tm, tn), lambda i, j: (i, j)),
        out_shape=jax.ShapeDtypeStruct((L, dff), BF16),
        compiler_params=_params(("parallel", "arbitrary"), tm * D * 4, 2 * D * tn * 2, tm * tn * 2,
                                tm * tn * 4),
        name="ffn_gate_up",
    )(x, nw, wgu, wgu)


def _ffn_down_body(a_ref, w_ref, res_ref, nw_ref, o_ref, *, final_norm):
    y = res_ref[...] + _dot(a_ref[...], w_ref[...])
    o_ref[...] = _rms(y, nw_ref[...]) if final_norm else y


def _ffn_down(a, w, layer, res, nw, final_norm, tm=512):
    L, D = res.shape
    dff = a.shape[1]
    return pl.pallas_call(
        functools.partial(_ffn_down_body, final_norm=final_norm),
        grid=(L // tm,),
        in_specs=[pl.BlockSpec((tm, dff), lambda i: (i, 0)),
                  pl.BlockSpec((None, dff, D), lambda i: (layer, 0, 0)),
                  pl.BlockSpec((tm, D), lambda i: (i, 0)),
                  pl.BlockSpec((1, D), lambda i: (0, 0))],
        out_specs=pl.BlockSpec((tm, D), lambda i: (i, 0)),
        out_shape=jax.ShapeDtypeStruct((L, D), F32),
        compiler_params=_params(("parallel",), tm * dff * 2, dff * D * 2, 2 * tm * D * 4),
        name="ffn_down",
    )(a, w, res, nw)


def _fourier1_body(x_ref, nw_ref, tab_ref, y_ref):
    nb = tab_ref.shape[0]
    n1 = x_ref.shape[0]
    D = nw_ref.shape[1]
    for j in range(nb):
        h = _rms(x_ref[:, j * D:(j + 1) * D], nw_ref[...]).astype(BF16)
        y = _dot(tab_ref[j], h)
        y_ref[0, j] = y[0:n1].astype(BF16)
        y_ref[1, j] = y[n1:].astype(BF16)


def _fourier2_body(f_ref, y_ref, z_ref):
    n2, nb, D = z_ref.shape[1], z_ref.shape[2], z_ref.shape[3]
    for j in range(nb):
        z = _dot(f_ref[...], y_ref[:, j * D:(j + 1) * D]).astype(BF16)
        z_ref[0, :, j, :] = z[0:n2]
        z_ref[1, :, j, :] = z[n2:]


def _fourier_out_body(pr_ref, pi_ref, cc_ref, sc_ref, w_ref, b_ref, res_ref, o_ref):
    pieces = []
    for g in range(N_FOURIER_GROUPS):
        sl = slice(g * GROUP_DIM, (g + 1) * GROUP_DIM)
        m = _dot(pr_ref[:, sl], cc_ref[...]) + _dot(pi_ref[:, sl], sc_ref[...])
        pieces.append(m.astype(BF16))
    o_ref[...] = res_ref[...] + (_dot(jnp.concatenate(pieces, axis=1), w_ref[...]) + b_ref[...])


def _dft_tables(L, n1, n2):
    two_pi = 2.0 * math.pi
    i2 = lax.broadcasted_iota(jnp.int32, (n2, n1, 1), 0)
    k1 = lax.broadcasted_iota(jnp.int32, (n2, n1, 1), 1)
    ang_a = ((k1 * i2) % L).astype(F32) * (two_pi / L)
    k1 = lax.broadcasted_iota(jnp.int32, (1, n1, n1), 1)
    i1 = lax.broadcasted_iota(jnp.int32, (1, n1, n1), 2)
    ang_b = ((k1 * i1) % n1).astype(F32) * (two_pi / n1)
    s1 = 1.0 / math.sqrt(n1)
    ca, sa, cb, sb = jnp.cos(ang_a) * s1, jnp.sin(ang_a) * s1, jnp.cos(ang_b), jnp.sin(ang_b)
    tab1 = jnp.concatenate([ca * cb - sa * sb, -(sa * cb + ca * sb)], axis=1).astype(BF16)
    a = lax.broadcasted_iota(jnp.int32, (n2, n2), 0)
    b = lax.broadcasted_iota(jnp.int32, (n2, n2), 1)
    ang2 = ((a * b) % n2).astype(F32) * (two_pi / n2)
    s2 = 1.0 / math.sqrt(n2)
    c2, sn2 = jnp.cos(ang2) * s2, jnp.sin(ang2) * s2
    tab2 = jnp.concatenate([jnp.concatenate([c2, sn2], axis=1),
                            jnp.concatenate([-sn2, c2], axis=1)], axis=0).astype(BF16)
    a = lax.broadcasted_iota(jnp.int32, (GROUP_DIM, GROUP_DIM), 0)
    b = lax.broadcasted_iota(jnp.int32, (GROUP_DIM, GROUP_DIM), 1)
    angc = ((a * b) % GROUP_DIM).astype(F32) * (two_pi / GROUP_DIM)
    sc = 1.0 / math.sqrt(GROUP_DIM)
    return tab1, tab2, (jnp.cos(angc) * sc).astype(BF16), (jnp.sin(angc) * sc).astype(BF16)


def _fourier_mixer(x, nw, wout, bout, nb=4, tm=512):
    L, D = x.shape
    n1 = n2 = math.isqrt(L)
    assert n1 * n2 == L
    tab1, tab2, cc, sc = _dft_tables(L, n1, n2)
    y = pl.pallas_call(
        _fourier1_body,
        grid=(n2 // nb,),
        in_specs=[pl.BlockSpec((n1, nb * D), lambda i: (0, i)),
                  pl.BlockSpec((1, D), lambda i: (0, 0)),
                  pl.BlockSpec((nb, 2 * n1, n1), lambda i: (i, 0, 0))],
        out_specs=pl.BlockSpec((2, nb, n1, D), lambda i: (0, i, 0, 0)),
        out_shape=jax.ShapeDtypeStruct((2, n2, n1, D), BF16),
        compiler_params=_params(("parallel",), n1 * nb * D * 4, nb * 2 * n1 * n1 * 2, 2 * nb * n1 * D * 2),
        name="fourier_stage1",
    )(x.reshape(n1, n2 * D), nw, tab1)
    nb2 = 16
    z = pl.pallas_call(
        _fourier2_body,
        grid=(n1 // nb2,),
        in_specs=[pl.BlockSpec((2 * n2, 2 * n2), lambda i: (0, 0)),
                  pl.BlockSpec((2 * n2, nb2 * D), lambda i: (0, i))],
        out_specs=pl.BlockSpec((2, n2, nb2, D), lambda i: (0, 0, i, 0)),
        out_shape=jax.ShapeDtypeStruct((2, n2, n1, D), BF16),
        compiler_params=_params(("parallel",), 2 * 2 * n2 * nb2 * D * 2),
        name="fourier_stage2",
    )(tab2, y.reshape(2 * n2, n1 * D))
    p = z.reshape(2, L, D)
    return pl.pallas_call(
        _fourier_out_body,
        grid=(L // tm,),
        in_specs=[pl.BlockSpec((None, tm, D), lambda i: (0, i, 0)),
                  pl.BlockSpec((None, tm, D), lambda i: (1, i, 0)),
                  pl.BlockSpec((GROUP_DIM, GROUP_DIM), lambda i: (0, 0)),
                  pl.BlockSpec((GROUP_DIM, GROUP_DIM), lambda i: (0, 0)),
                  pl.BlockSpec((D, D), lambda i: (0, 0)),
                  pl.BlockSpec((1, D), lambda i: (0, 0)),
                  pl.BlockSpec((tm, D), lambda i: (i, 0))],
        out_specs=pl.BlockSpec((tm, D), lambda i: (i, 0)),
        out_shape=jax.ShapeDtypeStruct((L, D), F32),
        compiler_params=_params(("parallel",), 2 * tm * D * 2, D * D * 2, 2 * tm * D * 4),
        name="fourier_out_proj",
    )(p, p, cc, sc, wout, bout, x)


def _kh_columns(a):
    lead = a.shape[:-1]
    q = a[..., :KEY_DIM].reshape(*lead, N_K_HEADS, HEAD_DIM)
    k = a[..., KEY_DIM:2 * KEY_DIM].reshape(*lead, N_K_HEADS, HEAD_DIM)
    v = a[..., 2 * KEY_DIM:].reshape(*lead, N_K_HEADS, 2 * HEAD_DIM)
    return jnp.concatenate([q, k, v], axis=-1).reshape(*lead, CONV_DIM)


def _gate_column_order():
    cols = []
    for h in range(N_K_HEADS):
        units = [d * N_V_HEADS + 2 * h + s for d in range(N_DIRS) for s in range(2)]
        cols += [N_DIRS * N_V_HEADS + un for un in units] + units
    return jnp.asarray(cols, jnp.int32)


def _unit_rows(p):
    per = p.reshape(N_DIRS, N_K_HEADS, 2).transpose(1, 0, 2).reshape(N_K_HEADS, 2 * N_DIRS)
    per = jnp.concatenate([per, jnp.zeros_like(per)], axis=1)
    return jnp.broadcast_to(per[:, :, None], (N_K_HEADS, N_GATE_ROWS, LANES_V7X)).astype(F32)


def _deltanet_layer(x, nw, w_in, conv_w, a_log, dt_bias, out_norm_w, w_out):
    L = x.shape[0]
    nc = L // CHUNK
    gate0 = CONV_DIM + VALUE_DIM
    wqkv = _kh_columns(w_in[:, :CONV_DIM]).astype(BF16)
    wz = w_in[:, CONV_DIM:gate0].astype(BF16)
    n_gate = 2 * N_DIRS * N_V_HEADS
    wg = jnp.pad(w_in[:, gate0:][:, _gate_column_order()].T, ((0, LANES_V7X - n_gate), (0, 0))).astype(BF16)
    cw = jnp.pad(_kh_columns(conv_w), ((0, SUBLANES_V7X - CONV_WIDTH), (0, 0)))

    y, z, gates = _in_proj(x, nw, wqkv, cw, wz, wg)
    w, u, qd, qk, kdt, gl = _dn_prep(y, gates, _unit_rows(a_log), _unit_rows(dt_bias))
    nu = N_DIRS * N_K_HEADS
    o = _dn_scan(w.reshape(nu, L, -1), u.reshape(nu, L, -1), qd.reshape(nu, L, -1), qk.reshape(nu, L, -1),
                 kdt.reshape(nu, nc * 2 * CHUNK, -1), gl.reshape(nu, nc, -1))
    return _dn_out(o, z, out_norm_w.reshape(1, -1), w_out.astype(BF16), x)


def kernel(x, mix_norm_w, ffn_norm_w, dn_w_in, dn_conv_w, dn_a_log, dn_dt_bias, dn_out_norm_w, dn_w_out,
           fn_w_out, fn_b_out, ffn_w_gate_up, ffn_w_down, final_norm_w):
    B, L, D = x.shape
    row = lambda v: v.reshape(1, -1)
    wgu, wdn = ffn_w_gate_up.astype(BF16), ffn_w_down.astype(BF16)
    outs = []
    for b in range(B):
        h = x[b]
        h = _deltanet_layer(h, row(mix_norm_w[0]), dn_w_in[0], dn_conv_w[0], dn_a_log[0], dn_dt_bias[0],
                            dn_out_norm_w[0], dn_w_out[0])
        a = _ffn_up(h, row(ffn_norm_w[0]), wgu, 0)
        h = _ffn_down(a, wdn, 0, h, row(final_norm_w), final_norm=False)
        h = _fourier_mixer(h, row(mix_norm_w[1]), fn_w_out[0].astype(BF16), row(fn_b_out[0]))
        a = _ffn_up(h, row(ffn_norm_w[1]), wgu, 1)
        h = _ffn_down(a, wdn, 1, h, row(final_norm_w), final_norm=True)
        outs.append(h)
    return jnp.stack(outs, axis=0)
```

```python
import functools
import math

import jax
import jax.numpy as jnp
from jax import lax
from jax.experimental import pallas as pl
from jax.experimental.pallas import tpu as pltpu

F32 = jnp.float32
BF16 = jnp.bfloat16

LANES_V7X = 128
SUBLANES_V7X = 8
VMEM_BYTES_V7X = 64 * 1024 * 1024
VMEM_LIMIT_CAP = VMEM_BYTES_V7X - 8 * 1024 * 1024

D_MODEL = 1024
N_K_HEADS = 8
N_V_HEADS = 16
HEAD_DIM = 128
KEY_DIM = N_K_HEADS * HEAD_DIM
VALUE_DIM = N_V_HEADS * HEAD_DIM
CONV_DIM = 2 * KEY_DIM + VALUE_DIM
CONV_WIDTH = 5
CONV_HALO = 16
CHUNK = 64
N_DIRS = 2
KH_COLS = 2 * HEAD_DIM + 2 * HEAD_DIM
N_GATE_ROWS = 8
N_FOURIER_GROUPS = 4
GROUP_DIM = D_MODEL // N_FOURIER_GROUPS
D_FF = 2816
RMS_EPS = 1e-6
L2_EPS = 1e-6

PREP_TILE = 2048
DN_TILE = 512
SCAN_GROUP = 8


def _params(semantics, *block_bytes):
    need = 2 * sum(block_bytes) + 24 * 1024 * 1024
    return pltpu.CompilerParams(dimension_semantics=semantics,
                                vmem_limit_bytes=int(min(need, VMEM_LIMIT_CAP)))


def _rms(x, w):
    return x * lax.rsqrt(jnp.mean(x * x, axis=-1, keepdims=True) + RMS_EPS) * w


def _silu(x):
    return x * jax.nn.sigmoid(x)


def _softplus(x):
    return jnp.maximum(x, 0.0) + jnp.log1p(jnp.exp(-jnp.abs(x)))


def _dot(a, b):
    return jnp.dot(a, b, preferred_element_type=F32)


def _dot_nt(a, b):
    return lax.dot_general(a, b, (((1,), (1,)), ((), ())), preferred_element_type=F32)


def _inproj_body(xm_ref, xp_ref, xn_ref, nw_ref, wqkv_ref, cw_ref, wz_ref, wg_ref,
                 qkv_ref, z_ref, g_ref, h_ref, p_ref):
    i, j = pl.program_id(0), pl.program_id(1)
    tm = xm_ref.shape[0]

    @pl.when(j == 0)
    def _():
        nw = nw_ref[...]
        hm = _rms(xm_ref[...], nw).astype(BF16)
        h_ref[0:CONV_HALO, :] = jnp.where(i > 0, _rms(xp_ref[...], nw), 0.0).astype(BF16)
        h_ref[CONV_HALO:CONV_HALO + tm, :] = hm
        h_ref[CONV_HALO + tm:, :] = jnp.where(i < pl.num_programs(0) - 1, _rms(xn_ref[...], nw), 0.0).astype(BF16)
        g_ref[...] = _dot_nt(wg_ref[...], hm)

    proj = _dot(h_ref[...], wqkv_ref[...])
    n_slabs = p_ref.shape[0]
    for c in range(n_slabs):
        p_ref[c] = proj[:, c * LANES_V7X:(c + 1) * LANES_V7X]
    first = CONV_HALO - (CONV_WIDTH - 1) // 2
    slabs_per_head = KH_COLS // LANES_V7X
    for c in range(n_slabs):
        lanes = slice(c * LANES_V7X, (c + 1) * LANES_V7X)
        y = p_ref[c, pl.ds(first, tm), :] * cw_ref[0:1, lanes]
        for tap in range(1, CONV_WIDTH):
            y = y + p_ref[c, pl.ds(first + tap, tm), :] * cw_ref[tap:tap + 1, lanes]
        y = _silu(y)
        role = c % slabs_per_head
        if role < 2:
            y = y * lax.rsqrt(jnp.sum(y * y, axis=-1, keepdims=True) + L2_EPS)
        if role == 0:
            y = y * (HEAD_DIM ** -0.5)
        qkv_ref[:, lanes] = y
    z_ref[...] = _dot(h_ref[CONV_HALO:CONV_HALO + tm, :], wz_ref[...]).astype(BF16)


def _in_proj(x, nw, wqkv, cw, wz, wg, tm=1024, nj=4):
    L, D = x.shape
    tq, tz = wqkv.shape[1] // nj, wz.shape[1] // nj
    hb = tm // CONV_HALO
    last_halo = L // CONV_HALO - 1
    return pl.pallas_call(
        _inproj_body,
        grid=(L // tm, nj),
        in_specs=[pl.BlockSpec((tm, D), lambda i, j: (i, 0)),
                  pl.BlockSpec((CONV_HALO, D), lambda i, j: (jnp.maximum(i * hb - 1, 0), 0)),
                  pl.BlockSpec((CONV_HALO, D), lambda i, j: (jnp.minimum((i + 1) * hb, last_halo), 0)),
                  pl.BlockSpec((1, D), lambda i, j: (0, 0)),
                  pl.BlockSpec((D, tq), lambda i, j: (0, j)),
                  pl.BlockSpec((SUBLANES_V7X, tq), lambda i, j: (0, j)),
                  pl.BlockSpec((D, tz), lambda i, j: (0, j)),
                  pl.BlockSpec((LANES_V7X, D), lambda i, j: (0, 0))],
        out_specs=[pl.BlockSpec((tm, tq), lambda i, j: (i, j)),
                   pl.BlockSpec((tm, tz), lambda i, j: (i, j)),
                   pl.BlockSpec((LANES_V7X, tm), lambda i, j: (0, i))],
        out_shape=[jax.ShapeDtypeStruct((L, wqkv.shape[1]), F32),
                   jax.ShapeDtypeStruct((L, wz.shape[1]), BF16),
                   jax.ShapeDtypeStruct((LANES_V7X, L), F32)],
        scratch_shapes=[pltpu.VMEM((tm + 2 * CONV_HALO, D), BF16),
                        pltpu.VMEM((tq // LANES_V7X, tm + 2 * CONV_HALO, LANES_V7X), F32)],
        compiler_params=_params(("parallel", "arbitrary"), tm * D * 4, D * tq * 2, D * tz * 2,
                                tm * tq * 4, tm * tz * 2, tm * LANES_V7X * 4, tm * D, tm * tq * 2),
        name="dn_in_proj",
    )(x, x, x, nw, wqkv, cw, wz, wg)


def _block_diag4(x):
    blk = lax.broadcasted_iota(jnp.int32, x.shape, 1) // CHUNK
    zero = jnp.zeros_like(x)
    return jnp.concatenate([jnp.where(blk == r, x, zero) for r in range(4)], axis=0)


def _block_diag2(x):
    left = lax.broadcasted_iota(jnp.int32, x.shape, 1) < HEAD_DIM
    zero = jnp.zeros_like(x)
    return jnp.concatenate([jnp.where(left, x, zero), jnp.where(left, zero, x)], axis=0)


def _dn_prep_body(y_ref, gr_ref, alog_ref, dtb_ref, w_ref, u_ref, qd_ref, qk_ref, kdt_ref, gl_ref):
    T = y_ref.shape[0]
    nch = T // CHUNK
    sls = [slice(c * CHUNK, (c + 1) * CHUNK) for c in range(nch)]
    q_of = lambda c: y_ref[sls[c], 0:HEAD_DIM]
    k_of = lambda c: y_ref[sls[c], HEAD_DIM:2 * HEAD_DIM]
    v_of = lambda c: y_ref[sls[c], 2 * HEAD_DIM:]

    rows = nch * N_GATE_ROWS
    half = lax.broadcasted_iota(jnp.int32, (N_GATE_ROWS, LANES_V7X), 1) < CHUNK
    slabs = []
    for j in range(T // LANES_V7X):
        two = gr_ref[:, j * LANES_V7X:(j + 1) * LANES_V7X]
        swapped = pltpu.roll(two, CHUNK, 1)
        slabs += [jnp.where(half, two, swapped), jnp.where(half, swapped, two)]
    gr = jnp.concatenate(slabs, axis=0)
    alog = jnp.concatenate([alog_ref[...]] * nch, axis=0)
    dtb = jnp.concatenate([dtb_ref[...]] * nch, axis=0)
    g = -jnp.exp(alog) * _softplus(gr + dtb)
    beta = jax.nn.sigmoid(gr)
    lane = lax.broadcasted_iota(jnp.int32, (rows, LANES_V7X), 1)
    l64 = lane % CHUNK
    unit = lax.broadcasted_iota(jnp.int32, (rows, LANES_V7X), 0) % N_GATE_ROWS
    pre, suf = g, g
    step = 1
    while step < CHUNK:
        pre = pre + jnp.where(l64 >= step, pltpu.roll(pre, step, 1), 0.0)
        suf = suf + jnp.where(l64 < CHUNK - step, pltpu.roll(suf, LANES_V7X - step, 1), 0.0)
        step *= 2
    cum = jnp.where(unit < 2, pre, suf)
    tot = pre + suf - g
    ecum = jnp.exp(cum)
    edec = jnp.exp(tot - cum)
    gl = jnp.exp(tot)

    cb = jnp.where(unit < 4, cum, beta)
    assert rows % LANES_V7X == 0
    cbt = [jnp.transpose(cb[b:b + LANES_V7X]) for b in range(0, rows, LANES_V7X)]

    def col(mats, c, r):
        blk, idx = divmod(c * N_GATE_ROWS + r, LANES_V7X)
        return jnp.broadcast_to(mats[blk][0:CHUNK, idx:idx + 1], (CHUNK, LANES_V7X))

    def row(mat, c, r):
        idx = c * N_GATE_ROWS + r
        return mat[idx:idx + 1, :]

    ri = lax.broadcasted_iota(jnp.int32, (CHUNK, LANES_V7X), 0)
    li = lax.broadcasted_iota(jnp.int32, (CHUNK, LANES_V7X), 1)
    left = li < CHUNK
    li64 = li % CHUNK
    eye4 = jnp.concatenate([(ri == li64).astype(F32)] * 2, axis=1)

    def qkk_of(c):
        kb = k_of(c).astype(BF16)
        return _dot_nt(jnp.concatenate([q_of(c).astype(BF16), kb], axis=0),
                       jnp.concatenate([kb, kb], axis=0))

    def decay_of(c, qkk):
        sl = sls[c]
        qc, kc = q_of(c), k_of(c)
        qk2, kk = qkk[0:CHUNK], qkk[CHUNK:]
        kt2 = jnp.transpose(jnp.concatenate([kc, kc], axis=0))
        n_pairs, beta_rows, ecum_rows = [], [], []
        for p in range(N_DIRS):
            r0, r1 = 2 * p, 2 * p + 1
            cum_row = jnp.where(left[0:1], row(cum, c, r0), row(cum, c, r1))
            col0, col1 = col(cbt, c, r0), col(cbt, c, r1)
            cum_col = jnp.where(left, col0, col1)
            beta_col = jnp.where(left, col(cbt, c, 4 + r0), col(cbt, c, 4 + r1))
            incl = (ri >= li64) if p == 0 else (ri <= li64)
            strict = (ri > li64) if p == 0 else (ri < li64)
            dm = jnp.where(incl, jnp.exp(jnp.where(incl, cum_col - cum_row, 0.0)), 0.0)
            n_pairs.append(jnp.where(strict, kk * dm * beta_col, 0.0))
            qk_ref[p, sl, :] = (qk2 * dm).astype(BF16)
            qd_ref[p, sl, :] = jnp.concatenate([qc * jnp.exp(col0), qc * jnp.exp(col1)], axis=1).astype(BF16)
            beta_rows.append(jnp.where(left[0:1], row(beta, c, 4 + r0), row(beta, c, 4 + r1)))
            ecum_rows.append(jnp.where(left[0:1], row(ecum, c, r0), row(ecum, c, r1)))
            edec_row = jnp.where(left[0:1], row(edec, c, r0), row(edec, c, r1))
            kdt_ref[p, c * 2 * CHUNK:(c + 1) * 2 * CHUNK, :] = (kt2 * edec_row).astype(BF16)
            gl_ref[p, c:c + 1, :] = jnp.concatenate([row(gl, c, r0), row(gl, c, r1)], axis=1)
        return (-jnp.concatenate(n_pairs, axis=1), jnp.concatenate(beta_rows, axis=1),
                jnp.concatenate(ecum_rows, axis=1))

    def solve_of(c, inv, beta_row, ecum_row):
        t1 = inv * beta_row
        t2 = t1 * ecum_row
        t1s = jnp.concatenate([t1[:, 0:LANES_V7X], t1[:, LANES_V7X:]], axis=0).astype(BF16)
        t2s = jnp.concatenate([t2[:, 0:LANES_V7X], t2[:, LANES_V7X:]], axis=0).astype(BF16)
        kb = k_of(c).astype(BF16)
        wv = _dot(t2s, _block_diag2(jnp.concatenate([kb, kb], axis=1)))
        uv = _dot(t1s, _block_diag2(v_of(c).astype(BF16)))
        for p in range(N_DIRS):
            w_ref[p, sls[c], :] = wv[p * CHUNK:(p + 1) * CHUNK].astype(BF16)
            u_ref[p, sls[c], :] = uv[p * CHUNK:(p + 1) * CHUNK].astype(BF16)

    def inverse_phases(pms):
        invs = [eye4 + pm for pm in pms]
        pbs = [pm.astype(BF16) for pm in pms]
        pws = [_dot(pb, _block_diag4(pb)) for pb in pbs]
        yield None
        for _ in range(4):
            pbs = [pw.astype(BF16) for pw in pws]
            ress = [_dot(jnp.concatenate([inv.astype(BF16), pb], axis=0), _block_diag4(pb))
                    for inv, pb in zip(invs, pbs)]
            invs = [inv + res[0:CHUNK] for inv, res in zip(invs, ress)]
            pws = [res[CHUNK:] for res in ress]
            yield None
        yield [inv + _dot(inv.astype(BF16), _block_diag4(pw.astype(BF16))) for inv, pw in zip(invs, pws)]

    def spread(items, n):
        return [items[len(items) * i // n:len(items) * (i + 1) // n] for i in range(n)]

    n_phases = 6
    half_a, half_b = list(range(nch // 2)), list(range(nch // 2, nch))
    dec = {c: decay_of(c, qkk_of(c)) for c in half_a}
    qkk_b = {c: qkk_of(c) for c in half_b}
    inv_a = None
    for piece, out in zip(spread(half_b, n_phases), inverse_phases([dec[c][0] for c in half_a])):
        inv_a = out
        for c in piece:
            dec[c] = decay_of(c, qkk_b[c])
    inv_b = None
    for piece, out in zip(spread(half_a, n_phases), inverse_phases([dec[c][0] for c in half_b])):
        inv_b = out
        for i in piece:
            solve_of(half_a[i], inv_a[i], dec[half_a[i]][1], dec[half_a[i]][2])
    for i, c in enumerate(half_b):
        solve_of(c, inv_b[i], dec[c][1], dec[c][2])


def _dn_prep(y, gates_t, alog_b, dtb_b, T=PREP_TILE):
    L = y.shape[0]
    nt, nch, nc = L // T, T // CHUNK, L // CHUNK
    pair = 2 * HEAD_DIM
    out_shapes = [jax.ShapeDtypeStruct((N_DIRS, N_K_HEADS, L, pair), BF16),
                  jax.ShapeDtypeStruct((N_DIRS, N_K_HEADS, L, pair), BF16),
                  jax.ShapeDtypeStruct((N_DIRS, N_K_HEADS, L, pair), BF16),
                  jax.ShapeDtypeStruct((N_DIRS, N_K_HEADS, L, HEAD_DIM), BF16),
                  jax.ShapeDtypeStruct((N_DIRS, N_K_HEADS, nc * 2 * CHUNK, HEAD_DIM), BF16),
                  jax.ShapeDtypeStruct((N_DIRS, N_K_HEADS, nc, pair), F32)]
    big = lambda h, i: (0, h, i, 0)
    return pl.pallas_call(
        _dn_prep_body,
        grid=(N_K_HEADS, nt),
        in_specs=[pl.BlockSpec((T, KH_COLS), lambda h, i: (i, h)),
                  pl.BlockSpec((N_GATE_ROWS, T), lambda h, i: (h, i)),
                  pl.BlockSpec((None, N_GATE_ROWS, LANES_V7X), lambda h, i: (h, 0, 0)),
                  pl.BlockSpec((None, N_GATE_ROWS, LANES_V7X), lambda h, i: (h, 0, 0))],
        out_specs=[pl.BlockSpec((N_DIRS, None, T, pair), big),
                   pl.BlockSpec((N_DIRS, None, T, pair), big),
                   pl.BlockSpec((N_DIRS, None, T, pair), big),
                   pl.BlockSpec((N_DIRS, None, T, HEAD_DIM), big),
                   pl.BlockSpec((N_DIRS, None, nch * 2 * CHUNK, HEAD_DIM), big),
                   pl.BlockSpec((N_DIRS, None, nch, pair), big)],
        out_shape=out_shapes,
        compiler_params=_params(("parallel", "parallel"), T * KH_COLS * 4, 3 * N_DIRS * T * pair * 2,
                                N_DIRS * T * HEAD_DIM * 2 * 2),
        name="dn_chunk_prep",
    )(y, gates_t, alog_b, dtb_b)


def _dn_scan_body(w_ref, u_ref, qd_ref, qk_ref, kdt_ref, gl_ref, o_ref, s_ref, *, n_groups):
    grp = pl.program_id(0)
    bwd = grp >= n_groups // N_DIRS
    G, T = w_ref.shape[0], w_ref.shape[1]
    nch = T // CHUNK

    @pl.when(pl.program_id(1) == 0)
    def _():
        s_ref[...] = jnp.zeros_like(s_ref)

    states = [s_ref[g] for g in range(G)]
    for c in range(nch):
        ce = jnp.where(bwd, nch - 1 - c, c)
        r0 = pl.multiple_of(ce * CHUNK, CHUNK)
        r1 = pl.multiple_of(ce * 2 * CHUNK, 2 * CHUNK)
        rows = pl.ds(r0, CHUNK)
        res_a = [_dot(jnp.concatenate([w_ref[g, rows, :], qd_ref[g, rows, :]], axis=0),
                      _block_diag2(states[g].astype(BF16))) for g in range(G)]
        v_new = [u_ref[g, rows, :].astype(F32) - res_a[g][0:CHUNK] for g in range(G)]
        res_b = [_dot(jnp.concatenate([qk_ref[g, rows, :], kdt_ref[g, pl.ds(r1, 2 * CHUNK), :]], axis=0),
                      _block_diag2(v_new[g].astype(BF16))) for g in range(G)]
        for g in range(G):
            o_ref[rows, g * 2 * HEAD_DIM:(g + 1) * 2 * HEAD_DIM] = (res_a[g][CHUNK:] + res_b[g][0:CHUNK]).astype(BF16)
            states[g] = states[g] * gl_ref[g, pl.ds(ce, 1), :] + res_b[g][CHUNK:]
    for g in range(G):
        s_ref[g] = states[g]


def _dn_scan(w, u, qd, qk, kdt, gl, T=DN_TILE, G=SCAN_GROUP):
    nu, L, pair = w.shape
    nt, nch = L // T, T // CHUNK
    n_groups = nu // G
    per_dir = n_groups // N_DIRS

    def tile(gi, i):
        return jnp.where(gi >= per_dir, nt - 1 - i, i)

    big = lambda gi, i: (gi, tile(gi, i), 0)
    return pl.pallas_call(
        functools.partial(_dn_scan_body, n_groups=n_groups),
        grid=(n_groups, nt),
        in_specs=[pl.BlockSpec((G, T, pair), big),
                  pl.BlockSpec((G, T, pair), big),
                  pl.BlockSpec((G, T, pair), big),
                  pl.BlockSpec((G, T, HEAD_DIM), big),
                  pl.BlockSpec((G, nch * 2 * CHUNK, HEAD_DIM), big),
                  pl.BlockSpec((G, nch, pair), big)],
        out_specs=pl.BlockSpec((None, T, G * pair), lambda gi, i: (gi // per_dir, tile(gi, i), gi % per_dir)),
        out_shape=jax.ShapeDtypeStruct((N_DIRS, L, VALUE_DIM), BF16),
        scratch_shapes=[pltpu.VMEM((G, HEAD_DIM, pair), F32)],
        compiler_params=_params(("parallel", "arbitrary"), 3 * G * T * pair * 2, 2 * G * T * HEAD_DIM * 2,
                                T * G * pair * 2),
        name="dn_state_scan",
    )(w, u, qd, qk, kdt, gl)


def _dn_out_body(of_ref, ob_ref, z_ref, nw_ref, wout_ref, res_ref, out_ref):
    nw = nw_ref[...]
    acc = res_ref[...]
    heads_per_dot = 2
    for h0 in range(0, N_V_HEADS, heads_per_dot):
        pieces = []
        for h in range(h0, h0 + heads_per_dot):
            sl = slice(h * HEAD_DIM, (h + 1) * HEAD_DIM)
            o = of_ref[:, sl].astype(F32) + ob_ref[:, sl].astype(F32)
            y = _rms(o, nw) * _silu(z_ref[:, sl].astype(F32))
            pieces.append(y.astype(BF16))
        rows = slice(h0 * HEAD_DIM, (h0 + heads_per_dot) * HEAD_DIM)
        acc = acc + _dot(jnp.concatenate(pieces, axis=1), wout_ref[rows, :])
    out_ref[...] = acc


def _dn_out(o, z, nw, wout, res, tm=512):
    L, D = res.shape
    V = z.shape[1]
    return pl.pallas_call(
        _dn_out_body,
        grid=(L // tm,),
        in_specs=[pl.BlockSpec((None, tm, V), lambda i: (0, i, 0)),
                  pl.BlockSpec((None, tm, V), lambda i: (1, i, 0)),
                  pl.BlockSpec((tm, V), lambda i: (i, 0)),
                  pl.BlockSpec((1, HEAD_DIM), lambda i: (0, 0)),
                  pl.BlockSpec((V, D), lambda i: (0, 0)),
                  pl.BlockSpec((tm, D), lambda i: (i, 0))],
        out_specs=pl.BlockSpec((tm, D), lambda i: (i, 0)),
        out_shape=jax.ShapeDtypeStruct((L, D), F32),
        compiler_params=_params(("parallel",), 3 * tm * V * 2, V * D * 2, 2 * tm * D * 4),
        name="dn_out_proj",
    )(o, o, z, nw, wout, res)


def _ffn_up_body(x_ref, nw_ref, wg_ref, wu_ref, a_ref):
    h = _rms(x_ref[...], nw_ref[...]).astype(BF16)
    a_ref[...] = (_silu(_dot(h, wg_ref[...])) * _dot(h, wu_ref[...])).astype(BF16)


def _ffn_up(x, nw, wgu, layer, tm=1024, nj=2):
    L, D = x.shape
    dff = wgu.shape[2] // 2
    tn = dff // nj
    return pl.pallas_call(
        _ffn_up_body,
        grid=(L // tm, nj),
        in_specs=[pl.BlockSpec((tm, D), lambda i, j: (i, 0)),
                  pl.BlockSpec((1, D), lambda i, j: (0, 0)),
                  pl.BlockSpec((None, D, tn), lambda i, j: (layer, 0, j)),
                  pl.BlockSpec((None, D, tn), lambda i, j: (layer, 0, nj + j))],
        out_specs=pl.BlockSpec((tm, tn), lambda i, j: (i, j)),
        out_shape=jax.ShapeDtypeStruct((L, dff), BF16),
        compiler_params=_params(("parallel", "arbitrary"), tm * D * 4, 2 * D * tn * 2, tm * tn * 2,
                                tm * tn * 4),
        name="ffn_gate_up",
    )(x, nw, wgu, wgu)


def _ffn_down_body(a_ref, w_ref, res_ref, nw_ref, o_ref, *h_ref, final_norm):
    y = res_ref[...] + _dot(a_ref[...], w_ref[...])
    if final_norm:
        o_ref[...] = _rms(y, nw_ref[...])
    else:
        o_ref[...] = y
    if h_ref:
        h_ref[0][...] = _rms(y, nw_ref[...]).astype(BF16)


def _ffn_down(a, w, layer, res, nw, final_norm, emit_norm=False, tm=512):
    L, D = res.shape
    dff = a.shape[1]
    out_specs = [pl.BlockSpec((tm, D), lambda i: (i, 0))]
    out_shape = [jax.ShapeDtypeStruct((L, D), F32)]
    if emit_norm:
        out_specs.append(pl.BlockSpec((tm, D), lambda i: (i, 0)))
        out_shape.append(jax.ShapeDtypeStruct((L, D), BF16))
    outs = pl.pallas_call(
        functools.partial(_ffn_down_body, final_norm=final_norm),
        grid=(L // tm,),
        in_specs=[pl.BlockSpec((tm, dff), lambda i: (i, 0)),
                  pl.BlockSpec((None, dff, D), lambda i: (layer, 0, 0)),
                  pl.BlockSpec((tm, D), lambda i: (i, 0)),
                  pl.BlockSpec((1, D), lambda i: (0, 0))],
        out_specs=out_specs,
        out_shape=out_shape,
        compiler_params=_params(("parallel",), tm * dff * 2, dff * D * 2, 3 * tm * D * 4),
        name="ffn_down",
    )(a, w, res, nw)
    return outs if emit_norm else outs[0]


def _fourier1_body(h_ref, tab_ref, y_ref):
    nb = tab_ref.shape[0]
    n1 = h_ref.shape[0]
    D = y_ref.shape[3]
    for j in range(nb):
        y = _dot(tab_ref[j], h_ref[:, j * D:(j + 1) * D])
        y_ref[0, j] = y[0:n1].astype(BF16)
        y_ref[1, j] = y[n1:].astype(BF16)


def _fourier2_body(f_ref, y_ref, z_ref):
    n2, nb, D = z_ref.shape[1], z_ref.shape[2], z_ref.shape[3]
    for j in range(nb):
        z = _dot(f_ref[...], y_ref[:, j * D:(j + 1) * D]).astype(BF16)
        z_ref[0, :, j, :] = z[0:n2]
        z_ref[1, :, j, :] = z[n2:]


def _fourier_out_body(pr_ref, pi_ref, cc_ref, sc_ref, w_ref, b_ref, res_ref, o_ref):
    pieces = []
    for g in range(N_FOURIER_GROUPS):
        sl = slice(g * GROUP_DIM, (g + 1) * GROUP_DIM)
        m = _dot(pr_ref[:, sl], cc_ref[...]) + _dot(pi_ref[:, sl], sc_ref[...])
        pieces.append(m.astype(BF16))
    o_ref[...] = res_ref[...] + (_dot(jnp.concatenate(pieces, axis=1), w_ref[...]) + b_ref[...])


def _dft_tables(L, n1, n2):
    two_pi = 2.0 * math.pi
    i2 = lax.broadcasted_iota(jnp.int32, (n2, n1, 1), 0)
    k1 = lax.broadcasted_iota(jnp.int32, (n2, n1, 1), 1)
    ang_a = ((k1 * i2) % L).astype(F32) * (two_pi / L)
    k1 = lax.broadcasted_iota(jnp.int32, (1, n1, n1), 1)
    i1 = lax.broadcasted_iota(jnp.int32, (1, n1, n1), 2)
    ang_b = ((k1 * i1) % n1).astype(F32) * (two_pi / n1)
    s1 = 1.0 / math.sqrt(n1)
    ca, sa, cb, sb = jnp.cos(ang_a) * s1, jnp.sin(ang_a) * s1, jnp.cos(ang_b), jnp.sin(ang_b)
    tab1 = jnp.concatenate([ca * cb - sa * sb, -(sa * cb + ca * sb)], axis=1).astype(BF16)
    a = lax.broadcasted_iota(jnp.int32, (n2, n2), 0)
    b = lax.broadcasted_iota(jnp.int32, (n2, n2), 1)
    ang2 = ((a * b) % n2).astype(F32) * (two_pi / n2)
    s2 = 1.0 / math.sqrt(n2)
    c2, sn2 = jnp.cos(ang2) * s2, jnp.sin(ang2) * s2
    tab2 = jnp.concatenate([jnp.concatenate([c2, sn2], axis=1),
                            jnp.concatenate([-sn2, c2], axis=1)], axis=0).astype(BF16)
    a = lax.broadcasted_iota(jnp.int32, (GROUP_DIM, GROUP_DIM), 0)
    b = lax.broadcasted_iota(jnp.int32, (GROUP_DIM, GROUP_DIM), 1)
    angc = ((a * b) % GROUP_DIM).astype(F32) * (two_pi / GROUP_DIM)
    sc = 1.0 / math.sqrt(GROUP_DIM)
    return tab1, tab2, (jnp.cos(angc) * sc).astype(BF16), (jnp.sin(angc) * sc).astype(BF16)


def _fourier_mixer(x, h, wout, bout, nb=4, tm=512):
    L, D = x.shape
    n1 = n2 = math.isqrt(L)
    assert n1 * n2 == L
    tab1, tab2, cc, sc = _dft_tables(L, n1, n2)
    y = pl.pallas_call(
        _fourier1_body,
        grid=(n2 // nb,),
        in_specs=[pl.BlockSpec((n1, nb * D), lambda i: (0, i)),
                  pl.BlockSpec((nb, 2 * n1, n1), lambda i: (i, 0, 0))],
        out_specs=pl.BlockSpec((2, nb, n1, D), lambda i: (0, i, 0, 0)),
        out_shape=jax.ShapeDtypeStruct((2, n2, n1, D), BF16),
        compiler_params=_params(("parallel",), n1 * nb * D * 2, nb * 2 * n1 * n1 * 2, 2 * nb * n1 * D * 2),
        name="fourier_stage1",
    )(h.reshape(n1, n2 * D), tab1)
    nb2 = 16
    z = pl.pallas_call(
        _fourier2_body,
        grid=(n1 // nb2,),
        in_specs=[pl.BlockSpec((2 * n2, 2 * n2), lambda i: (0, 0)),
                  pl.BlockSpec((2 * n2, nb2 * D), lambda i: (0, i))],
        out_specs=pl.BlockSpec((2, n2, nb2, D), lambda i: (0, 0, i, 0)),
        out_shape=jax.ShapeDtypeStruct((2, n2, n1, D), BF16),
        compiler_params=_params(("parallel",), 2 * 2 * n2 * nb2 * D * 2),
        name="fourier_stage2",
    )(tab2, y.reshape(2 * n2, n1 * D))
    p = z.reshape(2, L, D)
    return pl.pallas_call(
        _fourier_out_body,
        grid=(L // tm,),
        in_specs=[pl.BlockSpec((None, tm, D), lambda i: (0, i, 0)),
                  pl.BlockSpec((None, tm, D), lambda i: (1, i, 0)),
                  pl.BlockSpec((GROUP_DIM, GROUP_DIM), lambda i: (0, 0)),
                  pl.BlockSpec((GROUP_DIM, GROUP_DIM), lambda i: (0, 0)),
                  pl.BlockSpec((D, D), lambda i: (0, 0)),
                  pl.BlockSpec((1, D), lambda i: (0, 0)),
                  pl.BlockSpec((tm, D), lambda i: (i, 0))],
        out_specs=pl.BlockSpec((tm, D), lambda i: (i, 0)),
        out_shape=jax.ShapeDtypeStruct((L, D), F32),
        compiler_params=_params(("parallel",), 2 * tm * D * 2, D * D * 2, 2 * tm * D * 4),
        name="fourier_out_proj",
    )(p, p, cc, sc, wout, bout, x)


def _kh_columns(a):
    lead = a.shape[:-1]
    q = a[..., :KEY_DIM].reshape(*lead, N_K_HEADS, HEAD_DIM)
    k = a[..., KEY_DIM:2 * KEY_DIM].reshape(*lead, N_K_HEADS, HEAD_DIM)
    v = a[..., 2 * KEY_DIM:].reshape(*lead, N_K_HEADS, 2 * HEAD_DIM)
    return jnp.concatenate([q, k, v], axis=-1).reshape(*lead, CONV_DIM)


def _gate_column_order():
    cols = []
    for h in range(N_K_HEADS):
        units = [d * N_V_HEADS + 2 * h + s for d in range(N_DIRS) for s in range(2)]
        cols += [N_DIRS * N_V_HEADS + un for un in units] + units
    return jnp.asarray(cols, jnp.int32)


def _unit_rows(p):
    per = p.reshape(N_DIRS, N_K_HEADS, 2).transpose(1, 0, 2).reshape(N_K_HEADS, 2 * N_DIRS)
    per = jnp.concatenate([per, jnp.zeros_like(per)], axis=1)
    return jnp.broadcast_to(per[:, :, None], (N_K_HEADS, N_GATE_ROWS, LANES_V7X)).astype(F32)


def _deltanet_layer(x, nw, w_in, conv_w, a_log, dt_bias, out_norm_w, w_out):
    L = x.shape[0]
    nc = L // CHUNK
    gate0 = CONV_DIM + VALUE_DIM
    wqkv = _kh_columns(w_in[:, :CONV_DIM]).astype(BF16)
    wz = w_in[:, CONV_DIM:gate0].astype(BF16)
    n_gate = 2 * N_DIRS * N_V_HEADS
    wg = jnp.pad(w_in[:, gate0:][:, _gate_column_order()].T, ((0, LANES_V7X - n_gate), (0, 0))).astype(BF16)
    cw = jnp.pad(_kh_columns(conv_w), ((0, SUBLANES_V7X - CONV_WIDTH), (0, 0)))

    y, z, gates = _in_proj(x, nw, wqkv, cw, wz, wg)
    w, u, qd, qk, kdt, gl = _dn_prep(y, gates, _unit_rows(a_log), _unit_rows(dt_bias))
    nu = N_DIRS * N_K_HEADS
    o = _dn_scan(w.reshape(nu, L, -1), u.reshape(nu, L, -1), qd.reshape(nu, L, -1), qk.reshape(nu, L, -1),
                 kdt.reshape(nu, nc * 2 * CHUNK, -1), gl.reshape(nu, nc, -1))
    return _dn_out(o, z, out_norm_w.reshape(1, -1), w_out.astype(BF16), x)


def kernel(x, mix_norm_w, ffn_norm_w, dn_w_in, dn_conv_w, dn_a_log, dn_dt_bias, dn_out_norm_w, dn_w_out,
           fn_w_out, fn_b_out, ffn_w_gate_up, ffn_w_down, final_norm_w):
    B, L, D = x.shape
    row = lambda v: v.reshape(1, -1)
    wgu, wdn = ffn_w_gate_up.astype(BF16), ffn_w_down.astype(BF16)
    outs = []
    for b in range(B):
        h = x[b]
        h = _deltanet_layer(h, row(mix_norm_w[0]), dn_w_in[0], dn_conv_w[0], dn_a_log[0], dn_dt_bias[0],
                            dn_out_norm_w[0], dn_w_out[0])
        a = _ffn_up(h, row(ffn_norm_w[0]), wgu, 0)
        h, hn = _ffn_down(a, wdn, 0, h, row(mix_norm_w[1]), final_norm=False, emit_norm=True)
        h = _fourier_mixer(h, hn, fn_w_out[0].astype(BF16), row(fn_b_out[0]))
        a = _ffn_up(h, row(ffn_norm_w[1]), wgu, 1)
        h = _ffn_down(a, wdn, 1, h, row(final_norm_w), final_norm=True)
        outs.append(h)
    return jnp.stack(outs, axis=0)
```

```python
import functools
import math

import jax
import jax.numpy as jnp
from jax import lax
from jax.experimental import pallas as pl
from jax.experimental.pallas import tpu as pltpu

F32 = jnp.float32
BF16 = jnp.bfloat16

LANES_V7X = 128
SUBLANES_V7X = 8
VMEM_BYTES_V7X = 64 * 1024 * 1024
VMEM_LIMIT_CAP = VMEM_BYTES_V7X - 8 * 1024 * 1024

D_MODEL = 1024
N_K_HEADS = 8
N_V_HEADS = 16
HEAD_DIM = 128
KEY_DIM = N_K_HEADS * HEAD_DIM
VALUE_DIM = N_V_HEADS * HEAD_DIM
CONV_DIM = 2 * KEY_DIM + VALUE_DIM
CONV_WIDTH = 5
CONV_HALO = 16
CHUNK = 64
N_DIRS = 2
KH_COLS = 2 * HEAD_DIM + 2 * HEAD_DIM
N_GATE_ROWS = 8
N_FOURIER_GROUPS = 4
GROUP_DIM = D_MODEL // N_FOURIER_GROUPS
D_FF = 2816
RMS_EPS = 1e-6
L2_EPS = 1e-6

PREP_TILE = 2048
DN_TILE = 512
N_PREP_GROUPS = 4
SCAN_GROUP = 8


def _params(semantics, *block_bytes):
    need = 2 * sum(block_bytes) + 24 * 1024 * 1024
    return pltpu.CompilerParams(dimension_semantics=semantics,
                                vmem_limit_bytes=int(min(need, VMEM_LIMIT_CAP)))


def _rms(x, w):
    return x * lax.rsqrt(jnp.mean(x * x, axis=-1, keepdims=True) + RMS_EPS) * w


def _silu(x):
    return x * jax.nn.sigmoid(x)


def _softplus(x):
    return jnp.maximum(x, 0.0) + jnp.log1p(jnp.exp(-jnp.abs(x)))


def _dot(a, b):
    return jnp.dot(a, b, preferred_element_type=F32)


def _dot_nt(a, b):
    return lax.dot_general(a, b, (((1,), (1,)), ((), ())), preferred_element_type=F32)


def _inproj_body(xm_ref, xp_ref, xn_ref, nw_ref, wqkv_ref, cw_ref, wz_ref, wg_ref,
                 qkv_ref, z_ref, g_ref, h_ref, p_ref):
    i, j = pl.program_id(0), pl.program_id(1)
    tm = xm_ref.shape[0]

    @pl.when(j == 0)
    def _():
        nw = nw_ref[...]
        hm = _rms(xm_ref[...], nw).astype(BF16)
        h_ref[0:CONV_HALO, :] = jnp.where(i > 0, _rms(xp_ref[...], nw), 0.0).astype(BF16)
        h_ref[CONV_HALO:CONV_HALO + tm, :] = hm
        h_ref[CONV_HALO + tm:, :] = jnp.where(i < pl.num_programs(0) - 1, _rms(xn_ref[...], nw), 0.0).astype(BF16)
        g_ref[...] = _dot_nt(wg_ref[...], hm)

    proj = _dot(h_ref[...], wqkv_ref[...])
    n_slabs = p_ref.shape[0]
    for c in range(n_slabs):
        p_ref[c] = proj[:, c * LANES_V7X:(c + 1) * LANES_V7X]
    first = CONV_HALO - (CONV_WIDTH - 1) // 2
    slabs_per_head = KH_COLS // LANES_V7X
    for c in range(n_slabs):
        lanes = slice(c * LANES_V7X, (c + 1) * LANES_V7X)
        y = p_ref[c, pl.ds(first, tm), :] * cw_ref[0:1, lanes]
        for tap in range(1, CONV_WIDTH):
            y = y + p_ref[c, pl.ds(first + tap, tm), :] * cw_ref[tap:tap + 1, lanes]
        y = _silu(y)
        role = c % slabs_per_head
        if role < 2:
            y = y * lax.rsqrt(jnp.sum(y * y, axis=-1, keepdims=True) + L2_EPS)
        if role == 0:
            y = y * (HEAD_DIM ** -0.5)
        qkv_ref[:, lanes] = y
    z_ref[...] = _dot(h_ref[CONV_HALO:CONV_HALO + tm, :], wz_ref[...]).astype(BF16)


def _in_proj(x, nw, wqkv, cw, wz, wg, tm=1024, nj=4):
    L, D = x.shape
    tq, tz = wqkv.shape[1] // nj, wz.shape[1] // nj
    hb = tm // CONV_HALO
    last_halo = L // CONV_HALO - 1
    return pl.pallas_call(
        _inproj_body,
        grid=(L // tm, nj),
        in_specs=[pl.BlockSpec((tm, D), lambda i, j: (i, 0)),
                  pl.BlockSpec((CONV_HALO, D), lambda i, j: (jnp.maximum(i * hb - 1, 0), 0)),
                  pl.BlockSpec((CONV_HALO, D), lambda i, j: (jnp.minimum((i + 1) * hb, last_halo), 0)),
                  pl.BlockSpec((1, D), lambda i, j: (0, 0)),
                  pl.BlockSpec((D, tq), lambda i, j: (0, j)),
                  pl.BlockSpec((SUBLANES_V7X, tq), lambda i, j: (0, j)),
                  pl.BlockSpec((D, tz), lambda i, j: (0, j)),
                  pl.BlockSpec((LANES_V7X, D), lambda i, j: (0, 0))],
        out_specs=[pl.BlockSpec((tm, tq), lambda i, j: (i, j)),
                   pl.BlockSpec((tm, tz), lambda i, j: (i, j)),
                   pl.BlockSpec((LANES_V7X, tm), lambda i, j: (0, i))],
        out_shape=[jax.ShapeDtypeStruct((L, wqkv.shape[1]), F32),
                   jax.ShapeDtypeStruct((L, wz.shape[1]), BF16),
                   jax.ShapeDtypeStruct((LANES_V7X, L), F32)],
        scratch_shapes=[pltpu.VMEM((tm + 2 * CONV_HALO, D), BF16),
                        pltpu.VMEM((tq // LANES_V7X, tm + 2 * CONV_HALO, LANES_V7X), F32)],
        compiler_params=_params(("parallel", "arbitrary"), tm * D * 4, D * tq * 2, D * tz * 2,
                                tm * tq * 4, tm * tz * 2, tm * LANES_V7X * 4, tm * D, tm * tq * 2),
        name="dn_in_proj",
    )(x, x, x, nw, wqkv, cw, wz, wg)


def _block_diag4(x):
    blk = lax.broadcasted_iota(jnp.int32, x.shape, 1) // CHUNK
    zero = jnp.zeros_like(x)
    return jnp.concatenate([jnp.where(blk == r, x, zero) for r in range(4)], axis=0)


def _block_diag2(x):
    left = lax.broadcasted_iota(jnp.int32, x.shape, 1) < HEAD_DIM
    zero = jnp.zeros_like(x)
    return jnp.concatenate([jnp.where(left, x, zero), jnp.where(left, zero, x)], axis=0)


def _dn_prep_body(y_ref, gr_ref, alog_ref, dtb_ref, w_ref, u_ref, qd_ref, qk_ref, kdt_ref, gl_ref):
    T = y_ref.shape[0]
    nch = T // CHUNK
    sls = [slice(c * CHUNK, (c + 1) * CHUNK) for c in range(nch)]
    q_of = lambda c: y_ref[sls[c], 0:HEAD_DIM]
    k_of = lambda c: y_ref[sls[c], HEAD_DIM:2 * HEAD_DIM]
    v_of = lambda c: y_ref[sls[c], 2 * HEAD_DIM:]

    rows = nch * N_GATE_ROWS
    half = lax.broadcasted_iota(jnp.int32, (N_GATE_ROWS, LANES_V7X), 1) < CHUNK
    slabs = []
    for j in range(T // LANES_V7X):
        two = gr_ref[:, j * LANES_V7X:(j + 1) * LANES_V7X]
        swapped = pltpu.roll(two, CHUNK, 1)
        slabs += [jnp.where(half, two, swapped), jnp.where(half, swapped, two)]
    gr = jnp.concatenate(slabs, axis=0)
    alog = jnp.concatenate([alog_ref[...]] * nch, axis=0)
    dtb = jnp.concatenate([dtb_ref[...]] * nch, axis=0)
    g = -jnp.exp(alog) * _softplus(gr + dtb)
    beta = jax.nn.sigmoid(gr)
    lane = lax.broadcasted_iota(jnp.int32, (rows, LANES_V7X), 1)
    l64 = lane % CHUNK
    unit = lax.broadcasted_iota(jnp.int32, (rows, LANES_V7X), 0) % N_GATE_ROWS
    pre, suf = g, g
    step = 1
    while step < CHUNK:
        pre = pre + jnp.where(l64 >= step, pltpu.roll(pre, step, 1), 0.0)
        suf = suf + jnp.where(l64 < CHUNK - step, pltpu.roll(suf, LANES_V7X - step, 1), 0.0)
        step *= 2
    cum = jnp.where(unit < 2, pre, suf)
    tot = pre + suf - g
    ecum = jnp.exp(cum)
    edec = jnp.exp(tot - cum)
    gl = jnp.exp(tot)

    cb = jnp.where(unit < 4, cum, beta)
    assert rows % LANES_V7X == 0
    cbt = [jnp.transpose(cb[b:b + LANES_V7X]) for b in range(0, rows, LANES_V7X)]

    def col(mats, c, r):
        blk, idx = divmod(c * N_GATE_ROWS + r, LANES_V7X)
        return jnp.broadcast_to(mats[blk][0:CHUNK, idx:idx + 1], (CHUNK, LANES_V7X))

    def row(mat, c, r):
        idx = c * N_GATE_ROWS + r
        return mat[idx:idx + 1, :]

    ri = lax.broadcasted_iota(jnp.int32, (CHUNK, LANES_V7X), 0)
    li = lax.broadcasted_iota(jnp.int32, (CHUNK, LANES_V7X), 1)
    left = li < CHUNK
    li64 = li % CHUNK
    eye4 = jnp.concatenate([(ri == li64).astype(F32)] * 2, axis=1)

    def qkk_of(c):
        kb = k_of(c).astype(BF16)
        return _dot_nt(jnp.concatenate([q_of(c).astype(BF16), kb], axis=0),
                       jnp.concatenate([kb, kb], axis=0))

    def decay_of(c, qkk):
        sl = sls[c]
        qc, kc = q_of(c), k_of(c)
        qk2, kk = qkk[0:CHUNK], qkk[CHUNK:]
        kt2 = jnp.transpose(jnp.concatenate([kc, kc], axis=0))
        n_pairs, beta_rows, ecum_rows = [], [], []
        for p in range(N_DIRS):
            r0, r1 = 2 * p, 2 * p + 1
            cum_row = jnp.where(left[0:1], row(cum, c, r0), row(cum, c, r1))
            col0, col1 = col(cbt, c, r0), col(cbt, c, r1)
            cum_col = jnp.where(left, col0, col1)
            beta_col = jnp.where(left, col(cbt, c, 4 + r0), col(cbt, c, 4 + r1))
            incl = (ri >= li64) if p == 0 else (ri <= li64)
            strict = (ri > li64) if p == 0 else (ri < li64)
            dm = jnp.where(incl, jnp.exp(jnp.where(incl, cum_col - cum_row, 0.0)), 0.0)
            n_pairs.append(jnp.where(strict, kk * dm * beta_col, 0.0))
            qk_ref[p, sl, :] = (qk2 * dm).astype(BF16)
            qd_ref[p, sl, :] = jnp.concatenate([qc * jnp.exp(col0), qc * jnp.exp(col1)], axis=1).astype(BF16)
            beta_rows.append(jnp.where(left[0:1], row(beta, c, 4 + r0), row(beta, c, 4 + r1)))
            ecum_rows.append(jnp.where(left[0:1], row(ecum, c, r0), row(ecum, c, r1)))
            edec_row = jnp.where(left[0:1], row(edec, c, r0), row(edec, c, r1))
            kdt_ref[p, c * 2 * CHUNK:(c + 1) * 2 * CHUNK, :] = (kt2 * edec_row).astype(BF16)
            gl_ref[p, c:c + 1, :] = jnp.concatenate([row(gl, c, r0), row(gl, c, r1)], axis=1)
        return (-jnp.concatenate(n_pairs, axis=1), jnp.concatenate(beta_rows, axis=1),
                jnp.concatenate(ecum_rows, axis=1))

    def solve_of(c, inv, beta_row, ecum_row):
        t1 = inv * beta_row
        t2 = t1 * ecum_row
        t1s = jnp.concatenate([t1[:, 0:LANES_V7X], t1[:, LANES_V7X:]], axis=0).astype(BF16)
        t2s = jnp.concatenate([t2[:, 0:LANES_V7X], t2[:, LANES_V7X:]], axis=0).astype(BF16)
        kb = k_of(c).astype(BF16)
        wv = _dot(t2s, _block_diag2(jnp.concatenate([kb, kb], axis=1)))
        uv = _dot(t1s, _block_diag2(v_of(c).astype(BF16)))
        for p in range(N_DIRS):
            w_ref[p, sls[c], :] = wv[p * CHUNK:(p + 1) * CHUNK].astype(BF16)
            u_ref[p, sls[c], :] = uv[p * CHUNK:(p + 1) * CHUNK].astype(BF16)

    def inverse_phases(pms):
        invs = [eye4 + pm for pm in pms]
        pbs = [pm.astype(BF16) for pm in pms]
        pws = [_dot(pb, _block_diag4(pb)) for pb in pbs]
        yield None
        for _ in range(4):
            pbs = [pw.astype(BF16) for pw in pws]
            ress = [_dot(jnp.concatenate([inv.astype(BF16), pb], axis=0), _block_diag4(pb))
                    for inv, pb in zip(invs, pbs)]
            invs = [inv + res[0:CHUNK] for inv, res in zip(invs, ress)]
            pws = [res[CHUNK:] for res in ress]
            yield None
        yield [inv + _dot(inv.astype(BF16), _block_diag4(pw.astype(BF16))) for inv, pw in zip(invs, pws)]

    def spread(items, n):
        return [items[len(items) * i // n:len(items) * (i + 1) // n] for i in range(n)]

    n_phases = 6
    groups = spread(list(range(nch)), N_PREP_GROUPS)
    dec, invs = {}, {}
    for c in groups[0]:
        dec[c] = decay_of(c, qkk_of(c))
    for gi, grp in enumerate(groups):
        nxt = groups[gi + 1] if gi + 1 < len(groups) else []
        prv = groups[gi - 1] if gi > 0 else []
        qkk_n = {c: qkk_of(c) for c in nxt}
        fill_n, fill_p = spread(nxt, n_phases), spread(prv, n_phases)
        for ph, out in enumerate(inverse_phases([dec[c][0] for c in grp])):
            if out is not None:
                invs.update(zip(grp, out))
            for c in fill_p[ph]:
                solve_of(c, invs[c], dec[c][1], dec[c][2])
            for c in fill_n[ph]:
                dec[c] = decay_of(c, qkk_n[c])
    for c in groups[-1]:
        solve_of(c, invs[c], dec[c][1], dec[c][2])


def _dn_prep(y, gates_t, alog_b, dtb_b, T=PREP_TILE):
    L = y.shape[0]
    nt, nch, nc = L // T, T // CHUNK, L // CHUNK
    pair = 2 * HEAD_DIM
    out_shapes = [jax.ShapeDtypeStruct((N_DIRS, N_K_HEADS, L, pair), BF16),
                  jax.ShapeDtypeStruct((N_DIRS, N_K_HEADS, L, pair), BF16),
                  jax.ShapeDtypeStruct((N_DIRS, N_K_HEADS, L, pair), BF16),
                  jax.ShapeDtypeStruct((N_DIRS, N_K_HEADS, L, HEAD_DIM), BF16),
                  jax.ShapeDtypeStruct((N_DIRS, N_K_HEADS, nc * 2 * CHUNK, HEAD_DIM), BF16),
                  jax.ShapeDtypeStruct((N_DIRS, N_K_HEADS, nc, pair), F32)]
    big = lambda h, i: (0, h, i, 0)
    return pl.pallas_call(
        _dn_prep_body,
        grid=(N_K_HEADS, nt),
        in_specs=[pl.BlockSpec((T, KH_COLS), lambda h, i: (i, h)),
                  pl.BlockSpec((N_GATE_ROWS, T), lambda h, i: (h, i)),
                  pl.BlockSpec((None, N_GATE_ROWS, LANES_V7X), lambda h, i: (h, 0, 0)),
                  pl.BlockSpec((None, N_GATE_ROWS, LANES_V7X), lambda h, i: (h, 0, 0))],
        out_specs=[pl.BlockSpec((N_DIRS, None, T, pair), big),
                   pl.BlockSpec((N_DIRS, None, T, pair), big),
                   pl.BlockSpec((N_DIRS, None, T, pair), big),
                   pl.BlockSpec((N_DIRS, None, T, HEAD_DIM), big),
                   pl.BlockSpec((N_DIRS, None, nch * 2 * CHUNK, HEAD_DIM), big),
                   pl.BlockSpec((N_DIRS, None, nch, pair), big)],
        out_shape=out_shapes,
        compiler_params=_params(("parallel", "parallel"), T * KH_COLS * 4, 3 * N_DIRS * T * pair * 2,
                                N_DIRS * T * HEAD_DIM * 2 * 2),
        name="dn_chunk_prep",
    )(y, gates_t, alog_b, dtb_b)


def _dn_scan_body(w_ref, u_ref, qd_ref, qk_ref, kdt_ref, gl_ref, o_ref, s_ref, *, n_groups):
    grp = pl.program_id(0)
    bwd = grp >= n_groups // N_DIRS
    G, T = w_ref.shape[0], w_ref.shape[1]
    nch = T // CHUNK

    @pl.when(pl.program_id(1) == 0)
    def _():
        s_ref[...] = jnp.zeros_like(s_ref)

    states = [s_ref[g] for g in range(G)]
    for c in range(nch):
        ce = jnp.where(bwd, nch - 1 - c, c)
        r0 = pl.multiple_of(ce * CHUNK, CHUNK)
        r1 = pl.multiple_of(ce * 2 * CHUNK, 2 * CHUNK)
        rows = pl.ds(r0, CHUNK)
        res_a = [_dot(jnp.concatenate([w_ref[g, rows, :], qd_ref[g, rows, :]], axis=0),
                      _block_diag2(states[g].astype(BF16))) for g in range(G)]
        v_new = [u_ref[g, rows, :].astype(F32) - res_a[g][0:CHUNK] for g in range(G)]
        res_b = [_dot(jnp.concatenate([qk_ref[g, rows, :], kdt_ref[g, pl.ds(r1, 2 * CHUNK), :]], axis=0),
                      _block_diag2(v_new[g].astype(BF16))) for g in range(G)]
        for g in range(G):
            o_ref[rows, g * 2 * HEAD_DIM:(g + 1) * 2 * HEAD_DIM] = (res_a[g][CHUNK:] + res_b[g][0:CHUNK]).astype(BF16)
            states[g] = states[g] * gl_ref[g, pl.ds(ce, 1), :] + res_b[g][CHUNK:]
    for g in range(G):
        s_ref[g] = states[g]


def _dn_scan(w, u, qd, qk, kdt, gl, T=DN_TILE, G=SCAN_GROUP):
    nu, L, pair = w.shape
    nt, nch = L // T, T // CHUNK
    n_groups = nu // G
    per_dir = n_groups // N_DIRS

    def tile(gi, i):
        return jnp.where(gi >= per_dir, nt - 1 - i, i)

    big = lambda gi, i: (gi, tile(gi, i), 0)
    return pl.pallas_call(
        functools.partial(_dn_scan_body, n_groups=n_groups),
        grid=(n_groups, nt),
        in_specs=[pl.BlockSpec((G, T, pair), big),
                  pl.BlockSpec((G, T, pair), big),
                  pl.BlockSpec((G, T, pair), big),
                  pl.BlockSpec((G, T, HEAD_DIM), big),
                  pl.BlockSpec((G, nch * 2 * CHUNK, HEAD_DIM), big),
                  pl.BlockSpec((G, nch, pair), big)],
        out_specs=pl.BlockSpec((None, T, G * pair), lambda gi, i: (gi // per_dir, tile(gi, i), gi % per_dir)),
        out_shape=jax.ShapeDtypeStruct((N_DIRS, L, VALUE_DIM), BF16),
        scratch_shapes=[pltpu.VMEM((G, HEAD_DIM, pair), F32)],
        compiler_params=_params(("parallel", "arbitrary"), 3 * G * T * pair * 2, 2 * G * T * HEAD_DIM * 2,
                                T * G * pair * 2),
        name="dn_state_scan",
    )(w, u, qd, qk, kdt, gl)


def _dn_out_body(of_ref, ob_ref, z_ref, nw_ref, wout_ref, res_ref, out_ref):
    nw = nw_ref[...]
    acc = res_ref[...]
    heads_per_dot = 2
    for h0 in range(0, N_V_HEADS, heads_per_dot):
        pieces = []
        for h in range(h0, h0 + heads_per_dot):
            sl = slice(h * HEAD_DIM, (h + 1) * HEAD_DIM)
            o = of_ref[:, sl].astype(F32) + ob_ref[:, sl].astype(F32)
            y = _rms(o, nw) * _silu(z_ref[:, sl].astype(F32))
            pieces.append(y.astype(BF16))
        rows = slice(h0 * HEAD_DIM, (h0 + heads_per_dot) * HEAD_DIM)
        acc = acc + _dot(jnp.concatenate(pieces, axis=1), wout_ref[rows, :])
    out_ref[...] = acc


def _dn_out(o, z, nw, wout, res, tm=512):
    L, D = res.shape
    V = z.shape[1]
    return pl.pallas_call(
        _dn_out_body,
        grid=(L // tm,),
        in_specs=[pl.BlockSpec((None, tm, V), lambda i: (0, i, 0)),
                  pl.BlockSpec((None, tm, V), lambda i: (1, i, 0)),
                  pl.BlockSpec((tm, V), lambda i: (i, 0)),
                  pl.BlockSpec((1, HEAD_DIM), lambda i: (0, 0)),
                  pl.BlockSpec((V, D), lambda i: (0, 0)),
                  pl.BlockSpec((tm, D), lambda i: (i, 0))],
        out_specs=pl.BlockSpec((tm, D), lambda i: (i, 0)),
        out_shape=jax.ShapeDtypeStruct((L, D), F32),
        compiler_params=_params(("parallel",), 3 * tm * V * 2, V * D * 2, 2 * tm * D * 4),
        name="dn_out_proj",
    )(o, o, z, nw, wout, res)


def _ffn_up_body(x_ref, nw_ref, wg_ref, wu_ref, a_ref):
    h = _rms(x_ref[...], nw_ref[...]).astype(BF16)
    a_ref[...] = (_silu(_dot(h, wg_ref[...])) * _dot(h, wu_ref[...])).astype(BF16)


def _ffn_up(x, nw, wgu, layer, tm=512, nj=1):
    L, D = x.shape
    dff = wgu.shape[2] // 2
    tn = dff // nj
    return pl.pallas_call(
        _ffn_up_body,
        grid=(L // tm, nj),
        in_specs=[pl.BlockSpec((tm, D), lambda i, j: (i, 0)),
                  pl.BlockSpec((1, D), lambda i, j: (0, 0)),
                  pl.BlockSpec((None, D, tn), lambda i, j: (layer, 0, j)),
                  pl.BlockSpec((None, D, tn), lambda i, j: (layer, 0, nj + j))],
        out_specs=pl.BlockSpec((tm, tn), lambda i, j: (i, j)),
        out_shape=jax.ShapeDtypeStruct((L, dff), BF16),
        compiler_params=_params(("parallel", "arbitrary"), tm * D * 4, 2 * D * tn * 2, tm * tn * 2,
                                tm * tn * 4),
        name="ffn_gate_up",
    )(x, nw, wgu, wgu)


def _ffn_down_body(a_ref, w_ref, res_ref, nw_ref, o_ref, *h_ref, final_norm):
    y = res_ref[...] + _dot(a_ref[...], w_ref[...])
    if final_norm:
        o_ref[...] = _rms(y, nw_ref[...])
    else:
        o_ref[...] = y
    if h_ref:
        h_ref[0][...] = _rms(y, nw_ref[...]).astype(BF16)


def _ffn_down(a, w, layer, res, nw, final_norm, emit_norm=False, tm=512):
    L, D = res.shape
    dff = a.shape[1]
    out_specs = [pl.BlockSpec((tm, D), lambda i: (i, 0))]
    out_shape = [jax.ShapeDtypeStruct((L, D), F32)]
    if emit_norm:
        out_specs.append(pl.BlockSpec((tm, D), lambda i: (i, 0)))
        out_shape.append(jax.ShapeDtypeStruct((L, D), BF16))
    outs = pl.pallas_call(
        functools.partial(_ffn_down_body, final_norm=final_norm),
        grid=(L // tm,),
        in_specs=[pl.BlockSpec((tm, dff), lambda i: (i, 0)),
                  pl.BlockSpec((None, dff, D), lambda i: (layer, 0, 0)),
                  pl.BlockSpec((tm, D), lambda i: (i, 0)),
                  pl.BlockSpec((1, D), lambda i: (0, 0))],
        out_specs=out_specs,
        out_shape=out_shape,
        compiler_params=_params(("parallel",), tm * dff * 2, dff * D * 2, 3 * tm * D * 4),
        name="ffn_down",
    )(a, w, res, nw)
    return outs if emit_norm else outs[0]


def _fourier1_body(h_ref, tab_ref, y_ref):
    nb = tab_ref.shape[0]
    n1 = h_ref.shape[0]
    D = y_ref.shape[3]
    for j in range(nb):
        y = _dot(tab_ref[j], h_ref[:, j * D:(j + 1) * D])
        y_ref[0, j] = y[0:n1].astype(BF16)
        y_ref[1, j] = y[n1:].astype(BF16)


def _fourier2_body(f_ref, y_ref, z_ref):
    n2, nb, D = z_ref.shape[1], z_ref.shape[2], z_ref.shape[3]
    for j in range(nb):
        z = _dot(f_ref[...], y_ref[:, j * D:(j + 1) * D]).astype(BF16)
        z_ref[0, :, j, :] = z[0:n2]
        z_ref[1, :, j, :] = z[n2:]


def _fourier_out_body(pr_ref, pi_ref, cc_ref, sc_ref, w_ref, b_ref, res_ref, o_ref):
    pieces = []
    for g in range(N_FOURIER_GROUPS):
        sl = slice(g * GROUP_DIM, (g + 1) * GROUP_DIM)
        m = _dot(pr_ref[:, sl], cc_ref[...]) + _dot(pi_ref[:, sl], sc_ref[...])
        pieces.append(m.astype(BF16))
    o_ref[...] = res_ref[...] + (_dot(jnp.concatenate(pieces, axis=1), w_ref[...]) + b_ref[...])


def _dft_tables(L, n1, n2):
    two_pi = 2.0 * math.pi
    i2 = lax.broadcasted_iota(jnp.int32, (n2, n1, 1), 0)
    k1 = lax.broadcasted_iota(jnp.int32, (n2, n1, 1), 1)
    ang_a = ((k1 * i2) % L).astype(F32) * (two_pi / L)
    k1 = lax.broadcasted_iota(jnp.int32, (1, n1, n1), 1)
    i1 = lax.broadcasted_iota(jnp.int32, (1, n1, n1), 2)
    ang_b = ((k1 * i1) % n1).astype(F32) * (two_pi / n1)
    s1 = 1.0 / math.sqrt(n1)
    ca, sa, cb, sb = jnp.cos(ang_a) * s1, jnp.sin(ang_a) * s1, jnp.cos(ang_b), jnp.sin(ang_b)
    tab1 = jnp.concatenate([ca * cb - sa * sb, -(sa * cb + ca * sb)], axis=1).astype(BF16)
    a = lax.broadcasted_iota(jnp.int32, (n2, n2), 0)
    b = lax.broadcasted_iota(jnp.int32, (n2, n2), 1)
    ang2 = ((a * b) % n2).astype(F32) * (two_pi / n2)
    s2 = 1.0 / math.sqrt(n2)
    c2, sn2 = jnp.cos(ang2) * s2, jnp.sin(ang2) * s2
    tab2 = jnp.concatenate([jnp.concatenate([c2, sn2], axis=1),
                            jnp.concatenate([-sn2, c2], axis=1)], axis=0).astype(BF16)
    a = lax.broadcasted_iota(jnp.int32, (GROUP_DIM, GROUP_DIM), 0)
    b = lax.broadcasted_iota(jnp.int32, (GROUP_DIM, GROUP_DIM), 1)
    angc = ((a * b) % GROUP_DIM).astype(F32) * (two_pi / GROUP_DIM)
    sc = 1.0 / math.sqrt(GROUP_DIM)
    return tab1, tab2, (jnp.cos(angc) * sc).astype(BF16), (jnp.sin(angc) * sc).astype(BF16)


def _fourier_mixer(x, h, wout, bout, nb=4, tm=512):
    L, D = x.shape
    n1 = n2 = math.isqrt(L)
    assert n1 * n2 == L
    tab1, tab2, cc, sc = _dft_tables(L, n1, n2)
    y = pl.pallas_call(
        _fourier1_body,
        grid=(n2 // nb,),
        in_specs=[pl.BlockSpec((n1, nb * D), lambda i: (0, i)),
                  pl.BlockSpec((nb, 2 * n1, n1), lambda i: (i, 0, 0))],
        out_specs=pl.BlockSpec((2, nb, n1, D), lambda i: (0, i, 0, 0)),
        out_shape=jax.ShapeDtypeStruct((2, n2, n1, D), BF16),
        compiler_params=_params(("parallel",), n1 * nb * D * 2, nb * 2 * n1 * n1 * 2, 2 * nb * n1 * D * 2),
        name="fourier_stage1",
    )(h.reshape(n1, n2 * D), tab1)
    nb2 = 16
    z = pl.pallas_call(
        _fourier2_body,
        grid=(n1 // nb2,),
        in_specs=[pl.BlockSpec((2 * n2, 2 * n2), lambda i: (0, 0)),
                  pl.BlockSpec((2 * n2, nb2 * D), lambda i: (0, i))],
        out_specs=pl.BlockSpec((2, n2, nb2, D), lambda i: (0, 0, i, 0)),
        out_shape=jax.ShapeDtypeStruct((2, n2, n1, D), BF16),
        compiler_params=_params(("parallel",), 2 * 2 * n2 * nb2 * D * 2),
        name="fourier_stage2",
    )(tab2, y.reshape(2 * n2, n1 * D))
    p = z.reshape(2, L, D)
    return pl.pallas_call(
        _fourier_out_body,
        grid=(L // tm,),
        in_specs=[pl.BlockSpec((None, tm, D), lambda i: (0, i, 0)),
                  pl.BlockSpec((None, tm, D), lambda i: (1, i, 0)),
                  pl.BlockSpec((GROUP_DIM, GROUP_DIM), lambda i: (0, 0)),
                  pl.BlockSpec((GROUP_DIM, GROUP_DIM), lambda i: (0, 0)),
                  pl.BlockSpec((D, D), lambda i: (0, 0)),
                  pl.BlockSpec((1, D), lambda i: (0, 0)),
                  pl.BlockSpec((tm, D), lambda i: (i, 0))],
        out_specs=pl.BlockSpec((tm, D), lambda i: (i, 0)),
        out_shape=jax.ShapeDtypeStruct((L, D), F32),
        compiler_params=_params(("parallel",), 2 * tm * D * 2, D * D * 2, 2 * tm * D * 4),
        name="fourier_out_proj",
    )(p, p, cc, sc, wout, bout, x)


def _kh_columns(a):
    lead = a.shape[:-1]
    q = a[..., :KEY_DIM].reshape(*lead, N_K_HEADS, HEAD_DIM)
    k = a[..., KEY_DIM:2 * KEY_DIM].reshape(*lead, N_K_HEADS, HEAD_DIM)
    v = a[..., 2 * KEY_DIM:].reshape(*lead, N_K_HEADS, 2 * HEAD_DIM)
    return jnp.concatenate([q, k, v], axis=-1).reshape(*lead, CONV_DIM)


def _gate_column_order():
    cols = []
    for h in range(N_K_HEADS):
        units = [d * N_V_HEADS + 2 * h + s for d in range(N_DIRS) for s in range(2)]
        cols += [N_DIRS * N_V_HEADS + un for un in units] + units
    return jnp.asarray(cols, jnp.int32)


def _unit_rows(p):
    per = p.reshape(N_DIRS, N_K_HEADS, 2).transpose(1, 0, 2).reshape(N_K_HEADS, 2 * N_DIRS)
    per = jnp.concatenate([per, jnp.zeros_like(per)], axis=1)
    return jnp.broadcast_to(per[:, :, None], (N_K_HEADS, N_GATE_ROWS, LANES_V7X)).astype(F32)


def _deltanet_layer(x, nw, w_in, conv_w, a_log, dt_bias, out_norm_w, w_out):
    L = x.shape[0]
    nc = L // CHUNK
    gate0 = CONV_DIM + VALUE_DIM
    wqkv = _kh_columns(w_in[:, :CONV_DIM]).astype(BF16)
    wz = w_in[:, CONV_DIM:gate0].astype(BF16)
    n_gate = 2 * N_DIRS * N_V_HEADS
    wg = jnp.pad(w_in[:, gate0:][:, _gate_column_order()].T, ((0, LANES_V7X - n_gate), (0, 0))).astype(BF16)
    cw = jnp.pad(_kh_columns(conv_w), ((0, SUBLANES_V7X - CONV_WIDTH), (0, 0)))

    y, z, gates = _in_proj(x, nw, wqkv, cw, wz, wg)
    w, u, qd, qk, kdt, gl = _dn_prep(y, gates, _unit_rows(a_log), _unit_rows(dt_bias))
    nu = N_DIRS * N_K_HEADS
    o = _dn_scan(w.reshape(nu, L, -1), u.reshape(nu, L, -1), qd.reshape(nu, L, -1), qk.reshape(nu, L, -1),
                 kdt.reshape(nu, nc * 2 * CHUNK, -1), gl.reshape(nu, nc, -1))
    return _dn_out(o, z, out_norm_w.reshape(1, -1), w_out.astype(BF16), x)


def kernel(x, mix_norm_w, ffn_norm_w, dn_w_in, dn_conv_w, dn_a_log, dn_dt_bias, dn_out_norm_w, dn_w_out,
           fn_w_out, fn_b_out, ffn_w_gate_up, ffn_w_down, final_norm_w):
    B, L, D = x.shape
    row = lambda v: v.reshape(1, -1)
    wgu, wdn = ffn_w_gate_up.astype(BF16), ffn_w_down.astype(BF16)
    outs = []
    for b in range(B):
        h = x[b]
        h = _deltanet_layer(h, row(mix_norm_w[0]), dn_w_in[0], dn_conv_w[0], dn_a_log[0], dn_dt_bias[0],
                            dn_out_norm_w[0], dn_w_out[0])
        a = _ffn_up(h, row(ffn_norm_w[0]), wgu, 0)
        h, hn = _ffn_down(a, wdn, 0, h, row(mix_norm_w[1]), final_norm=False, emit_norm=True)
        h = _fourier_mixer(h, hn, fn_w_out[0].astype(BF16), row(fn_b_out[0]))
        a = _ffn_up(h, row(ffn_norm_w[1]), wgu, 1)
        h = _ffn_down(a, wdn, 1, h, row(final_norm_w), final_norm=True)
        outs.append(h)
    return jnp.stack(outs, axis=0)
```

```python
import functools
import math

import jax
import jax.numpy as jnp
from jax import lax
from jax.experimental import pallas as pl
from jax.experimental.pallas import tpu as pltpu

F32 = jnp.float32
BF16 = jnp.bfloat16

LANES_V7X = 128
SUBLANES_V7X = 8
VMEM_BYTES_V7X = 64 * 1024 * 1024
VMEM_LIMIT_CAP = VMEM_BYTES_V7X - 8 * 1024 * 1024

D_MODEL = 1024
N_K_HEADS = 8
N_V_HEADS = 16
HEAD_DIM = 128
KEY_DIM = N_K_HEADS * HEAD_DIM
VALUE_DIM = N_V_HEADS * HEAD_DIM
CONV_DIM = 2 * KEY_DIM + VALUE_DIM
CONV_WIDTH = 5
CONV_HALO = 16
CHUNK = 64
N_DIRS = 2
KH_COLS = 2 * HEAD_DIM + 2 * HEAD_DIM
N_GATE_ROWS = 8
N_FOURIER_GROUPS = 4
GROUP_DIM = D_MODEL // N_FOURIER_GROUPS
D_FF = 2816
RMS_EPS = 1e-6
L2_EPS = 1e-6

PREP_TILE = 2048
DN_TILE = 1024
N_PREP_GROUPS = 4
SCAN_GROUP = 8


def _params(semantics, *block_bytes):
    need = 2 * sum(block_bytes) + 24 * 1024 * 1024
    return pltpu.CompilerParams(dimension_semantics=semantics,
                                vmem_limit_bytes=int(min(need, VMEM_LIMIT_CAP)))


def _rms(x, w):
    return x * lax.rsqrt(jnp.mean(x * x, axis=-1, keepdims=True) + RMS_EPS) * w


def _silu(x):
    return x * jax.nn.sigmoid(x)


def _softplus(x):
    return jnp.maximum(x, 0.0) + jnp.log1p(jnp.exp(-jnp.abs(x)))


def _dot(a, b):
    return jnp.dot(a, b, preferred_element_type=F32)


def _dot_nt(a, b):
    return lax.dot_general(a, b, (((1,), (1,)), ((), ())), preferred_element_type=F32)


def _inproj_body(xm_ref, xp_ref, xn_ref, nw_ref, wqkv_ref, cw_ref, wz_ref, wg_ref,
                 qkv_ref, z_ref, g_ref, h_ref, p_ref):
    i, j = pl.program_id(0), pl.program_id(1)
    tm = xm_ref.shape[0]

    @pl.when(j == 0)
    def _():
        nw = nw_ref[...]
        hm = _rms(xm_ref[...], nw).astype(BF16)
        h_ref[0:CONV_HALO, :] = jnp.where(i > 0, _rms(xp_ref[...], nw), 0.0).astype(BF16)
        h_ref[CONV_HALO:CONV_HALO + tm, :] = hm
        h_ref[CONV_HALO + tm:, :] = jnp.where(i < pl.num_programs(0) - 1, _rms(xn_ref[...], nw), 0.0).astype(BF16)
        g_ref[...] = _dot_nt(wg_ref[...], hm)

    proj = _dot(h_ref[...], wqkv_ref[...])
    n_slabs = p_ref.shape[0]
    for c in range(n_slabs):
        p_ref[c] = proj[:, c * LANES_V7X:(c + 1) * LANES_V7X]
    first = CONV_HALO - (CONV_WIDTH - 1) // 2
    slabs_per_head = KH_COLS // LANES_V7X
    for c in range(n_slabs):
        lanes = slice(c * LANES_V7X, (c + 1) * LANES_V7X)
        y = p_ref[c, pl.ds(first, tm), :] * cw_ref[0:1, lanes]
        for tap in range(1, CONV_WIDTH):
            y = y + p_ref[c, pl.ds(first + tap, tm), :] * cw_ref[tap:tap + 1, lanes]
        y = _silu(y)
        role = c % slabs_per_head
        if role < 2:
            y = y * lax.rsqrt(jnp.sum(y * y, axis=-1, keepdims=True) + L2_EPS)
        if role == 0:
            y = y * (HEAD_DIM ** -0.5)
        qkv_ref[:, lanes] = y
    z_ref[...] = _dot(h_ref[CONV_HALO:CONV_HALO + tm, :], wz_ref[...]).astype(BF16)


def _in_proj(x, nw, wqkv, cw, wz, wg, tm=1024, nj=4):
    L, D = x.shape
    tq, tz = wqkv.shape[1] // nj, wz.shape[1] // nj
    hb = tm // CONV_HALO
    last_halo = L // CONV_HALO - 1
    return pl.pallas_call(
        _inproj_body,
        grid=(L // tm, nj),
        in_specs=[pl.BlockSpec((tm, D), lambda i, j: (i, 0)),
                  pl.BlockSpec((CONV_HALO, D), lambda i, j: (jnp.maximum(i * hb - 1, 0), 0)),
                  pl.BlockSpec((CONV_HALO, D), lambda i, j: (jnp.minimum((i + 1) * hb, last_halo), 0)),
                  pl.BlockSpec((1, D), lambda i, j: (0, 0)),
                  pl.BlockSpec((D, tq), lambda i, j: (0, j)),
                  pl.BlockSpec((SUBLANES_V7X, tq), lambda i, j: (0, j)),
                  pl.BlockSpec((D, tz), lambda i, j: (0, j)),
                  pl.BlockSpec((LANES_V7X, D), lambda i, j: (0, 0))],
        out_specs=[pl.BlockSpec((tm, tq), lambda i, j: (i, j)),
                   pl.BlockSpec((tm, tz), lambda i, j: (i, j)),
                   pl.BlockSpec((LANES_V7X, tm), lambda i, j: (0, i))],
        out_shape=[jax.ShapeDtypeStruct((L, wqkv.shape[1]), F32),
                   jax.ShapeDtypeStruct((L, wz.shape[1]), BF16),
                   jax.ShapeDtypeStruct((LANES_V7X, L), F32)],
        scratch_shapes=[pltpu.VMEM((tm + 2 * CONV_HALO, D), BF16),
                        pltpu.VMEM((tq // LANES_V7X, tm + 2 * CONV_HALO, LANES_V7X), F32)],
        compiler_params=_params(("parallel", "arbitrary"), tm * D * 4, D * tq * 2, D * tz * 2,
                                tm * tq * 4, tm * tz * 2, tm * LANES_V7X * 4, tm * D, tm * tq * 2),
        name="dn_in_proj",
    )(x, x, x, nw, wqkv, cw, wz, wg)


def _block_diag4(x):
    blk = lax.broadcasted_iota(jnp.int32, x.shape, 1) // CHUNK
    zero = jnp.zeros_like(x)
    return jnp.concatenate([jnp.where(blk == r, x, zero) for r in range(4)], axis=0)


def _block_diag2(x):
    left = lax.broadcasted_iota(jnp.int32, x.shape, 1) < HEAD_DIM
    zero = jnp.zeros_like(x)
    return jnp.concatenate([jnp.where(left, x, zero), jnp.where(left, zero, x)], axis=0)


def _dn_prep_body(y_ref, gr_ref, alog_ref, dtb_ref, w_ref, u_ref, qd_ref, qk_ref, kdt_ref, gl_ref):
    T = y_ref.shape[0]
    nch = T // CHUNK
    sls = [slice(c * CHUNK, (c + 1) * CHUNK) for c in range(nch)]
    q_of = lambda c: y_ref[sls[c], 0:HEAD_DIM]
    k_of = lambda c: y_ref[sls[c], HEAD_DIM:2 * HEAD_DIM]
    v_of = lambda c: y_ref[sls[c], 2 * HEAD_DIM:]

    rows = nch * N_GATE_ROWS
    half = lax.broadcasted_iota(jnp.int32, (N_GATE_ROWS, LANES_V7X), 1) < CHUNK
    slabs = []
    for j in range(T // LANES_V7X):
        two = gr_ref[:, j * LANES_V7X:(j + 1) * LANES_V7X]
        swapped = pltpu.roll(two, CHUNK, 1)
        slabs += [jnp.where(half, two, swapped), jnp.where(half, swapped, two)]
    gr = jnp.concatenate(slabs, axis=0)
    alog = jnp.concatenate([alog_ref[...]] * nch, axis=0)
    dtb = jnp.concatenate([dtb_ref[...]] * nch, axis=0)
    g = -jnp.exp(alog) * _softplus(gr + dtb)
    beta = jax.nn.sigmoid(gr)
    lane = lax.broadcasted_iota(jnp.int32, (rows, LANES_V7X), 1)
    l64 = lane % CHUNK
    unit = lax.broadcasted_iota(jnp.int32, (rows, LANES_V7X), 0) % N_GATE_ROWS
    pre, suf = g, g
    step = 1
    while step < CHUNK:
        pre = pre + jnp.where(l64 >= step, pltpu.roll(pre, step, 1), 0.0)
        suf = suf + jnp.where(l64 < CHUNK - step, pltpu.roll(suf, LANES_V7X - step, 1), 0.0)
        step *= 2
    cum = jnp.where(unit < 2, pre, suf)
    tot = pre + suf - g
    ecum = jnp.exp(cum)
    edec = jnp.exp(tot - cum)
    gl = jnp.exp(tot)

    cb = jnp.where(unit < 4, cum, beta)
    assert rows % LANES_V7X == 0
    cbt = [jnp.transpose(cb[b:b + LANES_V7X]) for b in range(0, rows, LANES_V7X)]

    def col(mats, c, r):
        blk, idx = divmod(c * N_GATE_ROWS + r, LANES_V7X)
        return jnp.broadcast_to(mats[blk][0:CHUNK, idx:idx + 1], (CHUNK, LANES_V7X))

    def row(mat, c, r):
        idx = c * N_GATE_ROWS + r
        return mat[idx:idx + 1, :]

    ri = lax.broadcasted_iota(jnp.int32, (CHUNK, LANES_V7X), 0)
    li = lax.broadcasted_iota(jnp.int32, (CHUNK, LANES_V7X), 1)
    left = li < CHUNK
    li64 = li % CHUNK
    eye4 = jnp.concatenate([(ri == li64).astype(F32)] * 2, axis=1)

    def qkk_of(c):
        kb = k_of(c).astype(BF16)
        return _dot_nt(jnp.concatenate([q_of(c).astype(BF16), kb], axis=0),
                       jnp.concatenate([kb, kb], axis=0))

    def decay_of(c, qkk):
        sl = sls[c]
        qc, kc = q_of(c), k_of(c)
        qk2, kk = qkk[0:CHUNK], qkk[CHUNK:]
        kt2 = jnp.transpose(jnp.concatenate([kc, kc], axis=0))
        n_pairs, beta_rows, ecum_rows = [], [], []
        for p in range(N_DIRS):
            r0, r1 = 2 * p, 2 * p + 1
            cum_row = jnp.where(left[0:1], row(cum, c, r0), row(cum, c, r1))
            col0, col1 = col(cbt, c, r0), col(cbt, c, r1)
            cum_col = jnp.where(left, col0, col1)
            beta_col = jnp.where(left, col(cbt, c, 4 + r0), col(cbt, c, 4 + r1))
            incl = (ri >= li64) if p == 0 else (ri <= li64)
            strict = (ri > li64) if p == 0 else (ri < li64)
            dm = jnp.where(incl, jnp.exp(jnp.where(incl, cum_col - cum_row, 0.0)), 0.0)
            n_pairs.append(jnp.where(strict, kk * dm * beta_col, 0.0))
            qk_ref[p, sl, :] = (qk2 * dm).astype(BF16)
            qd_ref[p, sl, :] = jnp.concatenate([qc * jnp.exp(col0), qc * jnp.exp(col1)], axis=1).astype(BF16)
            beta_rows.append(jnp.where(left[0:1], row(beta, c, 4 + r0), row(beta, c, 4 + r1)))
            ecum_rows.append(jnp.where(left[0:1], row(ecum, c, r0), row(ecum, c, r1)))
            edec_row = jnp.where(left[0:1], row(edec, c, r0), row(edec, c, r1))
            kdt_ref[p, c * 2 * CHUNK:(c + 1) * 2 * CHUNK, :] = (kt2 * edec_row).astype(BF16)
            gl_ref[p, c:c + 1, :] = jnp.concatenate([row(gl, c, r0), row(gl, c, r1)], axis=1)
        return (-jnp.concatenate(n_pairs, axis=1), jnp.concatenate(beta_rows, axis=1),
                jnp.concatenate(ecum_rows, axis=1))

    def solve_of(c, inv, beta_row, ecum_row):
        t1 = inv * beta_row
        t2 = t1 * ecum_row
        t1s = jnp.concatenate([t1[:, 0:LANES_V7X], t1[:, LANES_V7X:]], axis=0).astype(BF16)
        t2s = jnp.concatenate([t2[:, 0:LANES_V7X], t2[:, LANES_V7X:]], axis=0).astype(BF16)
        kb = k_of(c).astype(BF16)
        wv = _dot(t2s, _block_diag2(jnp.concatenate([kb, kb], axis=1)))
        uv = _dot(t1s, _block_diag2(v_of(c).astype(BF16)))
        for p in range(N_DIRS):
            w_ref[p, sls[c], :] = wv[p * CHUNK:(p + 1) * CHUNK].astype(BF16)
            u_ref[p, sls[c], :] = uv[p * CHUNK:(p + 1) * CHUNK].astype(BF16)

    def inverse_phases(pms):
        invs = [eye4 + pm for pm in pms]
        pbs = [pm.astype(BF16) for pm in pms]
        pws = [_dot(pb, _block_diag4(pb)) for pb in pbs]
        yield None
        for _ in range(4):
            pbs = [pw.astype(BF16) for pw in pws]
            ress = [_dot(jnp.concatenate([inv.astype(BF16), pb], axis=0), _block_diag4(pb))
                    for inv, pb in zip(invs, pbs)]
            invs = [inv + res[0:CHUNK] for inv, res in zip(invs, ress)]
            pws = [res[CHUNK:] for res in ress]
            yield None
        yield [inv + _dot(inv.astype(BF16), _block_diag4(pw.astype(BF16))) for inv, pw in zip(invs, pws)]

    def spread(items, n):
        return [items[len(items) * i // n:len(items) * (i + 1) // n] for i in range(n)]

    n_phases = 6
    groups = spread(list(range(nch)), N_PREP_GROUPS)
    dec, invs = {}, {}
    for c in groups[0]:
        dec[c] = decay_of(c, qkk_of(c))
    for gi, grp in enumerate(groups):
        nxt = groups[gi + 1] if gi + 1 < len(groups) else []
        prv = groups[gi - 1] if gi > 0 else []
        qkk_n = {c: qkk_of(c) for c in nxt}
        fill_n, fill_p = spread(nxt, n_phases), spread(prv, n_phases)
        for ph, out in enumerate(inverse_phases([dec[c][0] for c in grp])):
            if out is not None:
                invs.update(zip(grp, out))
            for c in fill_p[ph]:
                solve_of(c, invs[c], dec[c][1], dec[c][2])
            for c in fill_n[ph]:
                dec[c] = decay_of(c, qkk_n[c])
    for c in groups[-1]:
        solve_of(c, invs[c], dec[c][1], dec[c][2])


def _dn_prep(y, gates_t, alog_b, dtb_b, T=PREP_TILE):
    L = y.shape[0]
    nt, nch, nc = L // T, T // CHUNK, L // CHUNK
    pair = 2 * HEAD_DIM
    out_shapes = [jax.ShapeDtypeStruct((N_DIRS, N_K_HEADS, L, pair), BF16),
                  jax.ShapeDtypeStruct((N_DIRS, N_K_HEADS, L, pair), BF16),
                  jax.ShapeDtypeStruct((N_DIRS, N_K_HEADS, L, pair), BF16),
                  jax.ShapeDtypeStruct((N_DIRS, N_K_HEADS, L, HEAD_DIM), BF16),
                  jax.ShapeDtypeStruct((N_DIRS, N_K_HEADS, nc * 2 * CHUNK, HEAD_DIM), BF16),
                  jax.ShapeDtypeStruct((N_DIRS, N_K_HEADS, nc, pair), F32)]
    big = lambda h, i: (0, h, i, 0)
    return pl.pallas_call(
        _dn_prep_body,
        grid=(N_K_HEADS, nt),
        in_specs=[pl.BlockSpec((T, KH_COLS), lambda h, i: (i, h)),
                  pl.BlockSpec((N_GATE_ROWS, T), lambda h, i: (h, i)),
                  pl.BlockSpec((None, N_GATE_ROWS, LANES_V7X), lambda h, i: (h, 0, 0)),
                  pl.BlockSpec((None, N_GATE_ROWS, LANES_V7X), lambda h, i: (h, 0, 0))],
        out_specs=[pl.BlockSpec((N_DIRS, None, T, pair), big),
                   pl.BlockSpec((N_DIRS, None, T, pair), big),
                   pl.BlockSpec((N_DIRS, None, T, pair), big),
                   pl.BlockSpec((N_DIRS, None, T, HEAD_DIM), big),
                   pl.BlockSpec((N_DIRS, None, nch * 2 * CHUNK, HEAD_DIM), big),
                   pl.BlockSpec((N_DIRS, None, nch, pair), big)],
        out_shape=out_shapes,
        compiler_params=_params(("parallel", "parallel"), T * KH_COLS * 4, 3 * N_DIRS * T * pair * 2,
                                N_DIRS * T * HEAD_DIM * 2 * 2),
        name="dn_chunk_prep",
    )(y, gates_t, alog_b, dtb_b)


def _dn_scan_body(w_ref, u_ref, qd_ref, qk_ref, kdt_ref, gl_ref, o_ref, s_ref, *, n_groups):
    grp = pl.program_id(0)
    bwd = grp >= n_groups // N_DIRS
    G, T = w_ref.shape[0], w_ref.shape[1]
    nch = T // CHUNK

    @pl.when(pl.program_id(1) == 0)
    def _():
        s_ref[...] = jnp.zeros_like(s_ref)

    states = [s_ref[g] for g in range(G)]
    for c in range(nch):
        ce = jnp.where(bwd, nch - 1 - c, c)
        r0 = pl.multiple_of(ce * CHUNK, CHUNK)
        r1 = pl.multiple_of(ce * 2 * CHUNK, 2 * CHUNK)
        rows = pl.ds(r0, CHUNK)
        res_a = [_dot(jnp.concatenate([w_ref[g, rows, :], qd_ref[g, rows, :]], axis=0),
                      _block_diag2(states[g].astype(BF16))) for g in range(G)]
        v_new = [u_ref[g, rows, :].astype(F32) - res_a[g][0:CHUNK] for g in range(G)]
        res_b = [_dot(jnp.concatenate([qk_ref[g, rows, :], kdt_ref[g, pl.ds(r1, 2 * CHUNK), :]], axis=0),
                      _block_diag2(v_new[g].astype(BF16))) for g in range(G)]
        for g in range(G):
            o_ref[rows, g * 2 * HEAD_DIM:(g + 1) * 2 * HEAD_DIM] = (res_a[g][CHUNK:] + res_b[g][0:CHUNK]).astype(BF16)
            states[g] = states[g] * gl_ref[g, pl.ds(ce, 1), :] + res_b[g][CHUNK:]
    for g in range(G):
        s_ref[g] = states[g]


def _dn_scan(w, u, qd, qk, kdt, gl, T=DN_TILE, G=SCAN_GROUP):
    nu, L, pair = w.shape
    nt, nch = L // T, T // CHUNK
    n_groups = nu // G
    per_dir = n_groups // N_DIRS

    def tile(gi, i):
        return jnp.where(gi >= per_dir, nt - 1 - i, i)

    big = lambda gi, i: (gi, tile(gi, i), 0)
    return pl.pallas_call(
        functools.partial(_dn_scan_body, n_groups=n_groups),
        grid=(n_groups, nt),
        in_specs=[pl.BlockSpec((G, T, pair), big),
                  pl.BlockSpec((G, T, pair), big),
                  pl.BlockSpec((G, T, pair), big),
                  pl.BlockSpec((G, T, HEAD_DIM), big),
                  pl.BlockSpec((G, nch * 2 * CHUNK, HEAD_DIM), big),
                  pl.BlockSpec((G, nch, pair), big)],
        out_specs=pl.BlockSpec((None, T, G * pair), lambda gi, i: (gi // per_dir, tile(gi, i), gi % per_dir)),
        out_shape=jax.ShapeDtypeStruct((N_DIRS, L, VALUE_DIM), BF16),
        scratch_shapes=[pltpu.VMEM((G, HEAD_DIM, pair), F32)],
        compiler_params=_params(("parallel", "arbitrary"), 3 * G * T * pair * 2, 2 * G * T * HEAD_DIM * 2,
                                T * G * pair * 2),
        name="dn_state_scan",
    )(w, u, qd, qk, kdt, gl)


def _dn_out_body(of_ref, ob_ref, z_ref, nw_ref, wout_ref, res_ref, out_ref):
    nw = nw_ref[...]
    acc = res_ref[...]
    heads_per_dot = 2
    for h0 in range(0, N_V_HEADS, heads_per_dot):
        pieces = []
        for h in range(h0, h0 + heads_per_dot):
            sl = slice(h * HEAD_DIM, (h + 1) * HEAD_DIM)
            o = of_ref[:, sl].astype(F32) + ob_ref[:, sl].astype(F32)
            y = _rms(o, nw) * _silu(z_ref[:, sl].astype(F32))
            pieces.append(y.astype(BF16))
        rows = slice(h0 * HEAD_DIM, (h0 + heads_per_dot) * HEAD_DIM)
        acc = acc + _dot(jnp.concatenate(pieces, axis=1), wout_ref[rows, :])
    out_ref[...] = acc


def _dn_out(o, z, nw, wout, res, tm=512):
    L, D = res.shape
    V = z.shape[1]
    return pl.pallas_call(
        _dn_out_body,
        grid=(L // tm,),
        in_specs=[pl.BlockSpec((None, tm, V), lambda i: (0, i, 0)),
                  pl.BlockSpec((None, tm, V), lambda i: (1, i, 0)),
                  pl.BlockSpec((tm, V), lambda i: (i, 0)),
                  pl.BlockSpec((1, HEAD_DIM), lambda i: (0, 0)),
                  pl.BlockSpec((V, D), lambda i: (0, 0)),
                  pl.BlockSpec((tm, D), lambda i: (i, 0))],
        out_specs=pl.BlockSpec((tm, D), lambda i: (i, 0)),
        out_shape=jax.ShapeDtypeStruct((L, D), F32),
        compiler_params=_params(("parallel",), 3 * tm * V * 2, V * D * 2, 2 * tm * D * 4),
        name="dn_out_proj",
    )(o, o, z, nw, wout, res)


def _ffn_up_body(x_ref, nw_ref, wg_ref, wu_ref, a_ref):
    h = _rms(x_ref[...], nw_ref[...]).astype(BF16)
    a_ref[...] = (_silu(_dot(h, wg_ref[...])) * _dot(h, wu_ref[...])).astype(BF16)


def _ffn_up(x, nw, wgu, layer, tm=512, nj=1):
    L, D = x.shape
    dff = wgu.shape[2] // 2
    tn = dff // nj
    return pl.pallas_call(
        _ffn_up_body,
        grid=(L // tm, nj),
        in_specs=[pl.BlockSpec((tm, D), lambda i, j: (i, 0)),
                  pl.BlockSpec((1, D), lambda i, j: (0, 0)),
                  pl.BlockSpec((None, D, tn), lambda i, j: (layer, 0, j)),
                  pl.BlockSpec((None, D, tn), lambda i, j: (layer, 0, nj + j))],
        out_specs=pl.BlockSpec((tm, tn), lambda i, j: (i, j)),
        out_shape=jax.ShapeDtypeStruct((L, dff), BF16),
        compiler_params=_params(("parallel", "arbitrary"), tm * D * 4, 2 * D * tn * 2, tm * tn * 2,
                                tm * tn * 4),
        name="ffn_gate_up",
    )(x, nw, wgu, wgu)


def _ffn_down_body(a_ref, w_ref, res_ref, nw_ref, o_ref, *h_ref, final_norm):
    y = res_ref[...] + _dot(a_ref[...], w_ref[...])
    if final_norm:
        o_ref[...] = _rms(y, nw_ref[...])
    else:
        o_ref[...] = y
    if h_ref:
        h_ref[0][...] = _rms(y, nw_ref[...]).astype(BF16)


def _ffn_down(a, w, layer, res, nw, final_norm, emit_norm=False, tm=512):
    L, D = res.shape
    dff = a.shape[1]
    out_specs = [pl.BlockSpec((tm, D), lambda i: (i, 0))]
    out_shape = [jax.ShapeDtypeStruct((L, D), F32)]
    if emit_norm:
        out_specs.append(pl.BlockSpec((tm, D), lambda i: (i, 0)))
        out_shape.append(jax.ShapeDtypeStruct((L, D), BF16))
    outs = pl.pallas_call(
        functools.partial(_ffn_down_body, final_norm=final_norm),
        grid=(L // tm,),
        in_specs=[pl.BlockSpec((tm, dff), lambda i: (i, 0)),
                  pl.BlockSpec((None, dff, D), lambda i: (layer, 0, 0)),
                  pl.BlockSpec((tm, D), lambda i: (i, 0)),
                  pl.BlockSpec((1, D), lambda i: (0, 0))],
        out_specs=out_specs,
        out_shape=out_shape,
        compiler_params=_params(("parallel",), tm * dff * 2, dff * D * 2, 3 * tm * D * 4),
        name="ffn_down",
    )(a, w, res, nw)
    return outs if emit_norm else outs[0]


def _fourier1_body(h_ref, tab_ref, y_ref):
    nb = tab_ref.shape[0]
    n1 = h_ref.shape[0]
    D = y_ref.shape[3]
    for j in range(nb):
        y = _dot(tab_ref[j], h_ref[:, j * D:(j + 1) * D])
        y_ref[0, j] = y[0:n1].astype(BF16)
        y_ref[1, j] = y[n1:].astype(BF16)


def _fourier2_body(f_ref, y_ref, z_ref):
    n2, nb, D = z_ref.shape[1], z_ref.shape[2], z_ref.shape[3]
    for j in range(nb):
        z = _dot(f_ref[...], y_ref[:, j * D:(j + 1) * D]).astype(BF16)
        z_ref[0, :, j, :] = z[0:n2]
        z_ref[1, :, j, :] = z[n2:]


def _fourier_out_body(pr_ref, pi_ref, cc_ref, sc_ref, w_ref, b_ref, res_ref, o_ref):
    pieces = []
    for g in range(N_FOURIER_GROUPS):
        sl = slice(g * GROUP_DIM, (g + 1) * GROUP_DIM)
        m = _dot(pr_ref[:, sl], cc_ref[...]) + _dot(pi_ref[:, sl], sc_ref[...])
        pieces.append(m.astype(BF16))
    o_ref[...] = res_ref[...] + (_dot(jnp.concatenate(pieces, axis=1), w_ref[...]) + b_ref[...])


def _dft_tables(L, n1, n2):
    two_pi = 2.0 * math.pi
    i2 = lax.broadcasted_iota(jnp.int32, (n2, n1, 1), 0)
    k1 = lax.broadcasted_iota(jnp.int32, (n2, n1, 1), 1)
    ang_a = ((k1 * i2) % L).astype(F32) * (two_pi / L)
    k1 = lax.broadcasted_iota(jnp.int32, (1, n1, n1), 1)
    i1 = lax.broadcasted_iota(jnp.int32, (1, n1, n1), 2)
    ang_b = ((k1 * i1) % n1).astype(F32) * (two_pi / n1)
    s1 = 1.0 / math.sqrt(n1)
    ca, sa, cb, sb = jnp.cos(ang_a) * s1, jnp.sin(ang_a) * s1, jnp.cos(ang_b), jnp.sin(ang_b)
    tab1 = jnp.concatenate([ca * cb - sa * sb, -(sa * cb + ca * sb)], axis=1).astype(BF16)
    a = lax.broadcasted_iota(jnp.int32, (n2, n2), 0)
    b = lax.broadcasted_iota(jnp.int32, (n2, n2), 1)
    ang2 = ((a * b) % n2).astype(F32) * (two_pi / n2)
    s2 = 1.0 / math.sqrt(n2)
    c2, sn2 = jnp.cos(ang2) * s2, jnp.sin(ang2) * s2
    tab2 = jnp.concatenate([jnp.concatenate([c2, sn2], axis=1),
                            jnp.concatenate([-sn2, c2], axis=1)], axis=0).astype(BF16)
    a = lax.broadcasted_iota(jnp.int32, (GROUP_DIM, GROUP_DIM), 0)
    b = lax.broadcasted_iota(jnp.int32, (GROUP_DIM, GROUP_DIM), 1)
    angc = ((a * b) % GROUP_DIM).astype(F32) * (two_pi / GROUP_DIM)
    sc = 1.0 / math.sqrt(GROUP_DIM)
    return tab1, tab2, (jnp.cos(angc) * sc).astype(BF16), (jnp.sin(angc) * sc).astype(BF16)


def _fourier_mixer(x, h, wout, bout, nb=8, tm=1024):
    L, D = x.shape
    n1 = n2 = math.isqrt(L)
    assert n1 * n2 == L
    tab1, tab2, cc, sc = _dft_tables(L, n1, n2)
    y = pl.pallas_call(
        _fourier1_body,
        grid=(n2 // nb,),
        in_specs=[pl.BlockSpec((n1, nb * D), lambda i: (0, i)),
                  pl.BlockSpec((nb, 2 * n1, n1), lambda i: (i, 0, 0))],
        out_specs=pl.BlockSpec((2, nb, n1, D), lambda i: (0, i, 0, 0)),
        out_shape=jax.ShapeDtypeStruct((2, n2, n1, D), BF16),
        compiler_params=_params(("parallel",), n1 * nb * D * 2, nb * 2 * n1 * n1 * 2, 2 * nb * n1 * D * 2),
        name="fourier_stage1",
    )(h.reshape(n1, n2 * D), tab1)
    nb2 = 16
    z = pl.pallas_call(
        _fourier2_body,
        grid=(n1 // nb2,),
        in_specs=[pl.BlockSpec((2 * n2, 2 * n2), lambda i: (0, 0)),
                  pl.BlockSpec((2 * n2, nb2 * D), lambda i: (0, i))],
        out_specs=pl.BlockSpec((2, n2, nb2, D), lambda i: (0, 0, i, 0)),
        out_shape=jax.ShapeDtypeStruct((2, n2, n1, D), BF16),
        compiler_params=_params(("parallel",), 2 * 2 * n2 * nb2 * D * 2),
        name="fourier_stage2",
    )(tab2, y.reshape(2 * n2, n1 * D))
    p = z.reshape(2, L, D)
    return pl.pallas_call(
        _fourier_out_body,
        grid=(L // tm,),
        in_specs=[pl.BlockSpec((None, tm, D), lambda i: (0, i, 0)),
                  pl.BlockSpec((None, tm, D), lambda i: (1, i, 0)),
                  pl.BlockSpec((GROUP_DIM, GROUP_DIM), lambda i: (0, 0)),
                  pl.BlockSpec((GROUP_DIM, GROUP_DIM), lambda i: (0, 0)),
                  pl.BlockSpec((D, D), lambda i: (0, 0)),
                  pl.BlockSpec((1, D), lambda i: (0, 0)),
                  pl.BlockSpec((tm, D), lambda i: (i, 0))],
        out_specs=pl.BlockSpec((tm, D), lambda i: (i, 0)),
        out_shape=jax.ShapeDtypeStruct((L, D), F32),
        compiler_params=_params(("parallel",), 2 * tm * D * 2, D * D * 2, 2 * tm * D * 4),
        name="fourier_out_proj",
    )(p, p, cc, sc, wout, bout, x)


def _kh_columns(a):
    lead = a.shape[:-1]
    q = a[..., :KEY_DIM].reshape(*lead, N_K_HEADS, HEAD_DIM)
    k = a[..., KEY_DIM:2 * KEY_DIM].reshape(*lead, N_K_HEADS, HEAD_DIM)
    v = a[..., 2 * KEY_DIM:].reshape(*lead, N_K_HEADS, 2 * HEAD_DIM)
    return jnp.concatenate([q, k, v], axis=-1).reshape(*lead, CONV_DIM)


def _gate_column_order():
    cols = []
    for h in range(N_K_HEADS):
        units = [d * N_V_HEADS + 2 * h + s for d in range(N_DIRS) for s in range(2)]
        cols += [N_DIRS * N_V_HEADS + un for un in units] + units
    return jnp.asarray(cols, jnp.int32)


def _unit_rows(p):
    per = p.reshape(N_DIRS, N_K_HEADS, 2).transpose(1, 0, 2).reshape(N_K_HEADS, 2 * N_DIRS)
    per = jnp.concatenate([per, jnp.zeros_like(per)], axis=1)
    return jnp.broadcast_to(per[:, :, None], (N_K_HEADS, N_GATE_ROWS, LANES_V7X)).astype(F32)


def _deltanet_layer(x, nw, w_in, conv_w, a_log, dt_bias, out_norm_w, w_out):
    L = x.shape[0]
    nc = L // CHUNK
    gate0 = CONV_DIM + VALUE_DIM
    wqkv = _kh_columns(w_in[:, :CONV_DIM]).astype(BF16)
    wz = w_in[:, CONV_DIM:gate0].astype(BF16)
    n_gate = 2 * N_DIRS * N_V_HEADS
    wg = jnp.pad(w_in[:, gate0:][:, _gate_column_order()].T, ((0, LANES_V7X - n_gate), (0, 0))).astype(BF16)
    cw = jnp.pad(_kh_columns(conv_w), ((0, SUBLANES_V7X - CONV_WIDTH), (0, 0)))

    y, z, gates = _in_proj(x, nw, wqkv, cw, wz, wg)
    w, u, qd, qk, kdt, gl = _dn_prep(y, gates, _unit_rows(a_log), _unit_rows(dt_bias))
    nu = N_DIRS * N_K_HEADS
    o = _dn_scan(w.reshape(nu, L, -1), u.reshape(nu, L, -1), qd.reshape(nu, L, -1), qk.reshape(nu, L, -1),
                 kdt.reshape(nu, nc * 2 * CHUNK, -1), gl.reshape(nu, nc, -1))
    return _dn_out(o, z, out_norm_w.reshape(1, -1), w_out.astype(BF16), x)


def kernel(x, mix_norm_w, ffn_norm_w, dn_w_in, dn_conv_w, dn_a_log, dn_dt_bias, dn_out_norm_w, dn_w_out,
           fn_w_out, fn_b_out, ffn_w_gate_up, ffn_w_down, final_norm_w):
    B, L, D = x.shape
    row = lambda v: v.reshape(1, -1)
    wgu, wdn = ffn_w_gate_up.astype(BF16), ffn_w_down.astype(BF16)
    outs = []
    for b in range(B):
        h = x[b]
        h = _deltanet_layer(h, row(mix_norm_w[0]), dn_w_in[0], dn_conv_w[0], dn_a_log[0], dn_dt_bias[0],
                            dn_out_norm_w[0], dn_w_out[0])
        a = _ffn_up(h, row(ffn_norm_w[0]), wgu, 0)
        h, hn = _ffn_down(a, wdn, 0, h, row(mix_norm_w[1]), final_norm=False, emit_norm=True)
        h = _fourier_mixer(h, hn, fn_w_out[0].astype(BF16), row(fn_b_out[0]))
        a = _ffn_up(h, row(ffn_norm_w[1]), wgu, 1)
        h = _ffn_down(a, wdn, 1, h, row(final_norm_w), final_norm=True)
        outs.append(h)
    return jnp.stack(outs, axis=0)
```

```python
import functools
import math

import jax
import jax.numpy as jnp
from jax import lax
from jax.experimental import pallas as pl
from jax.experimental.pallas import tpu as pltpu

F32 = jnp.float32
BF16 = jnp.bfloat16

LANES_V7X = 128
SUBLANES_V7X = 8
VMEM_BYTES_V7X = 64 * 1024 * 1024
VMEM_LIMIT_CAP = VMEM_BYTES_V7X - 8 * 1024 * 1024

D_MODEL = 1024
N_K_HEADS = 8
N_V_HEADS = 16
HEAD_DIM = 128
KEY_DIM = N_K_HEADS * HEAD_DIM
VALUE_DIM = N_V_HEADS * HEAD_DIM
CONV_DIM = 2 * KEY_DIM + VALUE_DIM
CONV_WIDTH = 5
CONV_HALO = 16
CHUNK = 64
N_DIRS = 2
KH_COLS = 2 * HEAD_DIM + 2 * HEAD_DIM
N_GATE_ROWS = 8
N_FOURIER_GROUPS = 4
GROUP_DIM = D_MODEL // N_FOURIER_GROUPS
D_FF = 2816
RMS_EPS = 1e-6
L2_EPS = 1e-6

PREP_TILE = 2048
DN_TILE = 1024
N_PREP_GROUPS = 4
SCAN_GROUP = 8


def _params(semantics, *block_bytes):
    need = 2 * sum(block_bytes) + 24 * 1024 * 1024
    return pltpu.CompilerParams(dimension_semantics=semantics,
                                vmem_limit_bytes=int(min(need, VMEM_LIMIT_CAP)))


def _rms(x, w):
    return x * lax.rsqrt(jnp.mean(x * x, axis=-1, keepdims=True) + RMS_EPS) * w


def _silu(x):
    return x * jax.nn.sigmoid(x)


def _softplus(x):
    return jnp.maximum(x, 0.0) + jnp.log1p(jnp.exp(-jnp.abs(x)))


def _dot(a, b):
    return jnp.dot(a, b, preferred_element_type=F32)


def _dot_nt(a, b):
    return lax.dot_general(a, b, (((1,), (1,)), ((), ())), preferred_element_type=F32)


def _inproj_body(xm_ref, xp_ref, xn_ref, nw_ref, wqkv_ref, cw_ref, wz_ref, wg_ref,
                 qkv_ref, z_ref, g_ref, h_ref, p_ref):
    i, j = pl.program_id(0), pl.program_id(1)
    tm = xm_ref.shape[0]

    @pl.when(j == 0)
    def _():
        nw = nw_ref[...]
        hm = _rms(xm_ref[...], nw).astype(BF16)
        h_ref[0:CONV_HALO, :] = jnp.where(i > 0, _rms(xp_ref[...], nw), 0.0).astype(BF16)
        h_ref[CONV_HALO:CONV_HALO + tm, :] = hm
        h_ref[CONV_HALO + tm:, :] = jnp.where(i < pl.num_programs(0) - 1, _rms(xn_ref[...], nw), 0.0).astype(BF16)
        g_ref[...] = _dot_nt(wg_ref[...], hm)

    proj = _dot(h_ref[...], wqkv_ref[...])
    n_slabs = p_ref.shape[0]
    for c in range(n_slabs):
        p_ref[c] = proj[:, c * LANES_V7X:(c + 1) * LANES_V7X]
    first = CONV_HALO - (CONV_WIDTH - 1) // 2
    slabs_per_head = KH_COLS // LANES_V7X
    for c in range(n_slabs):
        lanes = slice(c * LANES_V7X, (c + 1) * LANES_V7X)
        y = p_ref[c, pl.ds(first, tm), :] * cw_ref[0:1, lanes]
        for tap in range(1, CONV_WIDTH):
            y = y + p_ref[c, pl.ds(first + tap, tm), :] * cw_ref[tap:tap + 1, lanes]
        y = _silu(y)
        role = c % slabs_per_head
        if role < 2:
            y = y * lax.rsqrt(jnp.sum(y * y, axis=-1, keepdims=True) + L2_EPS)
        if role == 0:
            y = y * (HEAD_DIM ** -0.5)
        qkv_ref[:, lanes] = y
    z_ref[...] = _dot(h_ref[CONV_HALO:CONV_HALO + tm, :], wz_ref[...]).astype(BF16)


def _in_proj(x, nw, wqkv, cw, wz, wg, tm=1024, nj=4):
    L, D = x.shape
    tq, tz = wqkv.shape[1] // nj, wz.shape[1] // nj
    hb = tm // CONV_HALO
    last_halo = L // CONV_HALO - 1
    return pl.pallas_call(
        _inproj_body,
        grid=(L // tm, nj),
        in_specs=[pl.BlockSpec((tm, D), lambda i, j: (i, 0)),
                  pl.BlockSpec((CONV_HALO, D), lambda i, j: (jnp.maximum(i * hb - 1, 0), 0)),
                  pl.BlockSpec((CONV_HALO, D), lambda i, j: (jnp.minimum((i + 1) * hb, last_halo), 0)),
                  pl.BlockSpec((1, D), lambda i, j: (0, 0)),
                  pl.BlockSpec((D, tq), lambda i, j: (0, j)),
                  pl.BlockSpec((SUBLANES_V7X, tq), lambda i, j: (0, j)),
                  pl.BlockSpec((D, tz), lambda i, j: (0, j)),
                  pl.BlockSpec((LANES_V7X, D), lambda i, j: (0, 0))],
        out_specs=[pl.BlockSpec((tm, tq), lambda i, j: (i, j)),
                   pl.BlockSpec((tm, tz), lambda i, j: (i, j)),
                   pl.BlockSpec((LANES_V7X, tm), lambda i, j: (0, i))],
        out_shape=[jax.ShapeDtypeStruct((L, wqkv.shape[1]), F32),
                   jax.ShapeDtypeStruct((L, wz.shape[1]), BF16),
                   jax.ShapeDtypeStruct((LANES_V7X, L), F32)],
        scratch_shapes=[pltpu.VMEM((tm + 2 * CONV_HALO, D), BF16),
                        pltpu.VMEM((tq // LANES_V7X, tm + 2 * CONV_HALO, LANES_V7X), F32)],
        compiler_params=_params(("parallel", "arbitrary"), tm * D * 4, D * tq * 2, D * tz * 2,
                                tm * tq * 4, tm * tz * 2, tm * LANES_V7X * 4, tm * D, tm * tq * 2),
        name="dn_in_proj",
    )(x, x, x, nw, wqkv, cw, wz, wg)


def _block_diag4(x):
    blk = lax.broadcasted_iota(jnp.int32, x.shape, 1) // CHUNK
    zero = jnp.zeros_like(x)
    return jnp.concatenate([jnp.where(blk == r, x, zero) for r in range(4)], axis=0)


def _block_diag2(x):
    left = lax.broadcasted_iota(jnp.int32, x.shape, 1) < HEAD_DIM
    zero = jnp.zeros_like(x)
    return jnp.concatenate([jnp.where(left, x, zero), jnp.where(left, zero, x)], axis=0)


def _dn_prep_body(y_ref, gr_ref, alog_ref, dtb_ref, w_ref, u_ref, qd_ref, qk_ref, kdt_ref, gl_ref):
    T = y_ref.shape[0]
    nch = T // CHUNK
    sls = [slice(c * CHUNK, (c + 1) * CHUNK) for c in range(nch)]
    q_of = lambda c: y_ref[sls[c], 0:HEAD_DIM]
    k_of = lambda c: y_ref[sls[c], HEAD_DIM:2 * HEAD_DIM]
    v_of = lambda c: y_ref[sls[c], 2 * HEAD_DIM:]

    rows = nch * N_GATE_ROWS
    half = lax.broadcasted_iota(jnp.int32, (N_GATE_ROWS, LANES_V7X), 1) < CHUNK
    slabs = []
    for j in range(T // LANES_V7X):
        two = gr_ref[:, j * LANES_V7X:(j + 1) * LANES_V7X]
        swapped = pltpu.roll(two, CHUNK, 1)
        slabs += [jnp.where(half, two, swapped), jnp.where(half, swapped, two)]
    gr = jnp.concatenate(slabs, axis=0)
    alog = jnp.concatenate([alog_ref[...]] * nch, axis=0)
    dtb = jnp.concatenate([dtb_ref[...]] * nch, axis=0)
    g = -jnp.exp(alog) * _softplus(gr + dtb)
    beta = jax.nn.sigmoid(gr)
    lane = lax.broadcasted_iota(jnp.int32, (rows, LANES_V7X), 1)
    l64 = lane % CHUNK
    unit = lax.broadcasted_iota(jnp.int32, (rows, LANES_V7X), 0) % N_GATE_ROWS
    pre, suf = g, g
    step = 1
    while step < CHUNK:
        pre = pre + jnp.where(l64 >= step, pltpu.roll(pre, step, 1), 0.0)
        suf = suf + jnp.where(l64 < CHUNK - step, pltpu.roll(suf, LANES_V7X - step, 1), 0.0)
        step *= 2
    cum = jnp.where(unit < 2, pre, suf)
    tot = pre + suf - g
    ecum = jnp.exp(cum)
    edec = jnp.exp(tot - cum)
    gl = jnp.exp(tot)

    cb = jnp.where(unit < 4, cum, beta)
    assert rows % LANES_V7X == 0
    cbt = [jnp.transpose(cb[b:b + LANES_V7X]) for b in range(0, rows, LANES_V7X)]

    def col(mats, c, r):
        blk, idx = divmod(c * N_GATE_ROWS + r, LANES_V7X)
        return jnp.broadcast_to(mats[blk][0:CHUNK, idx:idx + 1], (CHUNK, LANES_V7X))

    def row(mat, c, r):
        idx = c * N_GATE_ROWS + r
        return mat[idx:idx + 1, :]

    ri = lax.broadcasted_iota(jnp.int32, (CHUNK, LANES_V7X), 0)
    li = lax.broadcasted_iota(jnp.int32, (CHUNK, LANES_V7X), 1)
    left = li < CHUNK
    li64 = li % CHUNK
    eye4 = jnp.concatenate([(ri == li64).astype(F32)] * 2, axis=1)

    def qkk_of(c):
        kb = k_of(c).astype(BF16)
        return _dot_nt(jnp.concatenate([q_of(c).astype(BF16), kb], axis=0),
                       jnp.concatenate([kb, kb], axis=0))

    def decay_of(c, qkk):
        sl = sls[c]
        qc, kc = q_of(c), k_of(c)
        qk2, kk = qkk[0:CHUNK], qkk[CHUNK:]
        kt2 = jnp.transpose(jnp.concatenate([kc, kc], axis=0))
        n_pairs, beta_rows, ecum_rows = [], [], []
        for p in range(N_DIRS):
            r0, r1 = 2 * p, 2 * p + 1
            cum_row = jnp.where(left[0:1], row(cum, c, r0), row(cum, c, r1))
            col0, col1 = col(cbt, c, r0), col(cbt, c, r1)
            cum_col = jnp.where(left, col0, col1)
            beta_col = jnp.where(left, col(cbt, c, 4 + r0), col(cbt, c, 4 + r1))
            incl = (ri >= li64) if p == 0 else (ri <= li64)
            strict = (ri > li64) if p == 0 else (ri < li64)
            dm = jnp.where(incl, jnp.exp(jnp.where(incl, cum_col - cum_row, 0.0)), 0.0)
            n_pairs.append(jnp.where(strict, kk * dm * beta_col, 0.0))
            qk_ref[p, sl, :] = (qk2 * dm).astype(BF16)
            qd_ref[p, sl, :] = jnp.concatenate([qc * jnp.exp(col0), qc * jnp.exp(col1)], axis=1).astype(BF16)
            beta_rows.append(jnp.where(left[0:1], row(beta, c, 4 + r0), row(beta, c, 4 + r1)))
            ecum_rows.append(jnp.where(left[0:1], row(ecum, c, r0), row(ecum, c, r1)))
            edec_row = jnp.where(left[0:1], row(edec, c, r0), row(edec, c, r1))
            kdt_ref[p, c * 2 * CHUNK:(c + 1) * 2 * CHUNK, :] = (kt2 * edec_row).astype(BF16)
            gl_ref[p, c:c + 1, :] = jnp.concatenate([row(gl, c, r0), row(gl, c, r1)], axis=1)
        return (-jnp.concatenate(n_pairs, axis=1), jnp.concatenate(beta_rows, axis=1),
                jnp.concatenate(ecum_rows, axis=1))

    def solve_of(c, inv, beta_row, ecum_row):
        t1 = inv * beta_row
        t2 = t1 * ecum_row
        t1s = jnp.concatenate([t1[:, 0:LANES_V7X], t1[:, LANES_V7X:]], axis=0).astype(BF16)
        t2s = jnp.concatenate([t2[:, 0:LANES_V7X], t2[:, LANES_V7X:]], axis=0).astype(BF16)
        kb = k_of(c).astype(BF16)
        wv = _dot(t2s, _block_diag2(jnp.concatenate([kb, kb], axis=1)))
        uv = _dot(t1s, _block_diag2(v_of(c).astype(BF16)))
        for p in range(N_DIRS):
            w_ref[p, sls[c], :] = wv[p * CHUNK:(p + 1) * CHUNK].astype(BF16)
            u_ref[p, sls[c], :] = uv[p * CHUNK:(p + 1) * CHUNK].astype(BF16)

    def inverse_phases(pms):
        invs = [eye4 + pm for pm in pms]
        pbs = [pm.astype(BF16) for pm in pms]
        pws = [_dot(pb, _block_diag4(pb)) for pb in pbs]
        yield None
        for _ in range(4):
            pbs = [pw.astype(BF16) for pw in pws]
            ress = [_dot(jnp.concatenate([inv.astype(BF16), pb], axis=0), _block_diag4(pb))
                    for inv, pb in zip(invs, pbs)]
            invs = [inv + res[0:CHUNK] for inv, res in zip(invs, ress)]
            pws = [res[CHUNK:] for res in ress]
            yield None
        yield [inv + _dot(inv.astype(BF16), _block_diag4(pw.astype(BF16))) for inv, pw in zip(invs, pws)]

    def spread(items, n):
        return [items[len(items) * i // n:len(items) * (i + 1) // n] for i in range(n)]

    n_phases = 6
    groups = spread(list(range(nch)), N_PREP_GROUPS)
    dec, invs = {}, {}
    for c in groups[0]:
        dec[c] = decay_of(c, qkk_of(c))
    for gi, grp in enumerate(groups):
        nxt = groups[gi + 1] if gi + 1 < len(groups) else []
        prv = groups[gi - 1] if gi > 0 else []
        qkk_n = {c: qkk_of(c) for c in nxt}
        fill_n, fill_p = spread(nxt, n_phases), spread(prv, n_phases)
        for ph, out in enumerate(inverse_phases([dec[c][0] for c in grp])):
            if out is not None:
                invs.update(zip(grp, out))
            for c in fill_p[ph]:
                solve_of(c, invs[c], dec[c][1], dec[c][2])
            for c in fill_n[ph]:
                dec[c] = decay_of(c, qkk_n[c])
    for c in groups[-1]:
        solve_of(c, invs[c], dec[c][1], dec[c][2])


def _dn_prep(y, gates_t, alog_b, dtb_b, T=PREP_TILE):
    L = y.shape[0]
    nt, nch, nc = L // T, T // CHUNK, L // CHUNK
    pair = 2 * HEAD_DIM
    out_shapes = [jax.ShapeDtypeStruct((N_DIRS, N_K_HEADS, L, pair), BF16),
                  jax.ShapeDtypeStruct((N_DIRS, N_K_HEADS, L, pair), BF16),
                  jax.ShapeDtypeStruct((N_DIRS, N_K_HEADS, L, pair), BF16),
                  jax.ShapeDtypeStruct((N_DIRS, N_K_HEADS, L, HEAD_DIM), BF16),
                  jax.ShapeDtypeStruct((N_DIRS, N_K_HEADS, nc * 2 * CHUNK, HEAD_DIM), BF16),
                  jax.ShapeDtypeStruct((N_DIRS, N_K_HEADS, nc, pair), F32)]
    big = lambda h, i: (0, h, i, 0)
    return pl.pallas_call(
        _dn_prep_body,
        grid=(N_K_HEADS, nt),
        in_specs=[pl.BlockSpec((T, KH_COLS), lambda h, i: (i, h)),
                  pl.BlockSpec((N_GATE_ROWS, T), lambda h, i: (h, i)),
                  pl.BlockSpec((None, N_GATE_ROWS, LANES_V7X), lambda h, i: (h, 0, 0)),
                  pl.BlockSpec((None, N_GATE_ROWS, LANES_V7X), lambda h, i: (h, 0, 0))],
        out_specs=[pl.BlockSpec((N_DIRS, None, T, pair), big),
                   pl.BlockSpec((N_DIRS, None, T, pair), big),
                   pl.BlockSpec((N_DIRS, None, T, pair), big),
                   pl.BlockSpec((N_DIRS, None, T, HEAD_DIM), big),
                   pl.BlockSpec((N_DIRS, None, nch * 2 * CHUNK, HEAD_DIM), big),
                   pl.BlockSpec((N_DIRS, None, nch, pair), big)],
        out_shape=out_shapes,
        compiler_params=_params(("parallel", "parallel"), T * KH_COLS * 4, 3 * N_DIRS * T * pair * 2,
                                N_DIRS * T * HEAD_DIM * 2 * 2),
        name="dn_chunk_prep",
    )(y, gates_t, alog_b, dtb_b)


def _dn_scan_body(w_ref, u_ref, qd_ref, qk_ref, kdt_ref, gl_ref, o_ref, s_ref, *, n_groups):
    grp = pl.program_id(0)
    bwd = grp >= n_groups // N_DIRS
    G, T = w_ref.shape[0], w_ref.shape[1]
    nch = T // CHUNK

    @pl.when(pl.program_id(1) == 0)
    def _():
        s_ref[...] = jnp.zeros_like(s_ref)

    states = [s_ref[g] for g in range(G)]
    for c in range(nch):
        ce = jnp.where(bwd, nch - 1 - c, c)
        r0 = pl.multiple_of(ce * CHUNK, CHUNK)
        r1 = pl.multiple_of(ce * 2 * CHUNK, 2 * CHUNK)
        rows = pl.ds(r0, CHUNK)
        res_a = [_dot(jnp.concatenate([w_ref[g, rows, :], qd_ref[g, rows, :]], axis=0),
                      _block_diag2(states[g].astype(BF16))) for g in range(G)]
        v_new = [u_ref[g, rows, :].astype(F32) - res_a[g][0:CHUNK] for g in range(G)]
        res_b = [_dot(jnp.concatenate([qk_ref[g, rows, :], kdt_ref[g, pl.ds(r1, 2 * CHUNK), :]], axis=0),
                      _block_diag2(v_new[g].astype(BF16))) for g in range(G)]
        for g in range(G):
            o_ref[rows, g * 2 * HEAD_DIM:(g + 1) * 2 * HEAD_DIM] = (res_a[g][CHUNK:] + res_b[g][0:CHUNK]).astype(BF16)
            states[g] = states[g] * gl_ref[g, pl.ds(ce, 1), :] + res_b[g][CHUNK:]
    for g in range(G):
        s_ref[g] = states[g]


def _dn_scan(w, u, qd, qk, kdt, gl, T=DN_TILE, G=SCAN_GROUP):
    nu, L, pair = w.shape
    nt, nch = L // T, T // CHUNK
    n_groups = nu // G
    per_dir = n_groups // N_DIRS

    def tile(gi, i):
        return jnp.where(gi >= per_dir, nt - 1 - i, i)

    big = lambda gi, i: (gi, tile(gi, i), 0)
    return pl.pallas_call(
        functools.partial(_dn_scan_body, n_groups=n_groups),
        grid=(n_groups, nt),
        in_specs=[pl.BlockSpec((G, T, pair), big),
                  pl.BlockSpec((G, T, pair), big),
                  pl.BlockSpec((G, T, pair), big),
                  pl.BlockSpec((G, T, HEAD_DIM), big),
                  pl.BlockSpec((G, nch * 2 * CHUNK, HEAD_DIM), big),
                  pl.BlockSpec((G, nch, pair), big)],
        out_specs=pl.BlockSpec((None, T, G * pair), lambda gi, i: (gi // per_dir, tile(gi, i), gi % per_dir)),
        out_shape=jax.ShapeDtypeStruct((N_DIRS, L, VALUE_DIM), BF16),
        scratch_shapes=[pltpu.VMEM((G, HEAD_DIM, pair), F32)],
        compiler_params=_params(("parallel", "arbitrary"), 3 * G * T * pair * 2, 2 * G * T * HEAD_DIM * 2,
                                T * G * pair * 2),
        name="dn_state_scan",
    )(w, u, qd, qk, kdt, gl)


def _dn_out_body(of_ref, ob_ref, z_ref, nw_ref, wout_ref, res_ref, out_ref):
    nw = nw_ref[...]
    acc = res_ref[...]
    heads_per_dot = 2
    for h0 in range(0, N_V_HEADS, heads_per_dot):
        pieces = []
        for h in range(h0, h0 + heads_per_dot):
            sl = slice(h * HEAD_DIM, (h + 1) * HEAD_DIM)
            o = of_ref[:, sl].astype(F32) + ob_ref[:, sl].astype(F32)
            y = _rms(o, nw) * _silu(z_ref[:, sl].astype(F32))
            pieces.append(y.astype(BF16))
        rows = slice(h0 * HEAD_DIM, (h0 + heads_per_dot) * HEAD_DIM)
        acc = acc + _dot(jnp.concatenate(pieces, axis=1), wout_ref[rows, :])
    out_ref[...] = acc


def _dn_out(o, z, nw, wout, res, tm=512):
    L, D = res.shape
    V = z.shape[1]
    return pl.pallas_call(
        _dn_out_body,
        grid=(L // tm,),
        in_specs=[pl.BlockSpec((None, tm, V), lambda i: (0, i, 0)),
                  pl.BlockSpec((None, tm, V), lambda i: (1, i, 0)),
                  pl.BlockSpec((tm, V), lambda i: (i, 0)),
                  pl.BlockSpec((1, HEAD_DIM), lambda i: (0, 0)),
                  pl.BlockSpec((V, D), lambda i: (0, 0)),
                  pl.BlockSpec((tm, D), lambda i: (i, 0))],
        out_specs=pl.BlockSpec((tm, D), lambda i: (i, 0)),
        out_shape=jax.ShapeDtypeStruct((L, D), F32),
        compiler_params=_params(("parallel",), 3 * tm * V * 2, V * D * 2, 2 * tm * D * 4),
        name="dn_out_proj",
    )(o, o, z, nw, wout, res)


def _ffn_up_body(x_ref, nw_ref, wg_ref, wu_ref, a_ref):
    h = _rms(x_ref[...], nw_ref[...]).astype(BF16)
    a_ref[...] = (_silu(_dot(h, wg_ref[...])) * _dot(h, wu_ref[...])).astype(BF16)


def _ffn_up(x, nw, wgu, layer, tm=512, nj=1):
    L, D = x.shape
    dff = wgu.shape[2] // 2
    tn = dff // nj
    return pl.pallas_call(
        _ffn_up_body,
        grid=(L // tm, nj),
        in_specs=[pl.BlockSpec((tm, D), lambda i, j: (i, 0)),
                  pl.BlockSpec((1, D), lambda i, j: (0, 0)),
                  pl.BlockSpec((None, D, tn), lambda i, j: (layer, 0, j)),
                  pl.BlockSpec((None, D, tn), lambda i, j: (layer, 0, nj + j))],
        out_specs=pl.BlockSpec((tm, tn), lambda i, j: (i, j)),
        out_shape=jax.ShapeDtypeStruct((L, dff), BF16),
        compiler_params=_params(("parallel", "arbitrary"), tm * D * 4, 2 * D * tn * 2, tm * tn * 2,
                                tm * tn * 4),
        name="ffn_gate_up",
    )(x, nw, wgu, wgu)


def _ffn_down_body(a_ref, w_ref, res_ref, nw_ref, o_ref, *h_ref, final_norm):
    y = res_ref[...] + _dot(a_ref[...], w_ref[...].astype(BF16))
    if final_norm:
        o_ref[...] = _rms(y, nw_ref[...])
    else:
        o_ref[...] = y
    if h_ref:
        h_ref[0][...] = _rms(y, nw_ref[...]).astype(BF16)


def _ffn_down(a, w, layer, res, nw, final_norm, emit_norm=False, tm=512):
    L, D = res.shape
    dff = a.shape[1]
    out_specs = [pl.BlockSpec((tm, D), lambda i: (i, 0))]
    out_shape = [jax.ShapeDtypeStruct((L, D), F32)]
    if emit_norm:
        out_specs.append(pl.BlockSpec((tm, D), lambda i: (i, 0)))
        out_shape.append(jax.ShapeDtypeStruct((L, D), BF16))
    outs = pl.pallas_call(
        functools.partial(_ffn_down_body, final_norm=final_norm),
        grid=(L // tm,),
        in_specs=[pl.BlockSpec((tm, dff), lambda i: (i, 0)),
                  pl.BlockSpec((None, dff, D), lambda i: (layer, 0, 0)),
                  pl.BlockSpec((tm, D), lambda i: (i, 0)),
                  pl.BlockSpec((1, D), lambda i: (0, 0))],
        out_specs=out_specs,
        out_shape=out_shape,
        compiler_params=_params(("parallel",), tm * dff * 2, dff * D * 4, 3 * tm * D * 4),
        name="ffn_down",
    )(a, w, res, nw)
    return outs if emit_norm else outs[0]


def _fourier1_body(h_ref, tab_ref, y_ref):
    nb = tab_ref.shape[0]
    n1 = h_ref.shape[0]
    D = y_ref.shape[3]
    for j in range(nb):
        y = _dot(tab_ref[j], h_ref[:, j * D:(j + 1) * D])
        y_ref[0, j] = y[0:n1].astype(BF16)
        y_ref[1, j] = y[n1:].astype(BF16)


def _fourier2_body(f_ref, y_ref, z_ref):
    n2, nb, D = z_ref.shape[1], z_ref.shape[2], z_ref.shape[3]
    for j in range(nb):
        z = _dot(f_ref[...], y_ref[:, j * D:(j + 1) * D]).astype(BF16)
        z_ref[0, :, j, :] = z[0:n2]
        z_ref[1, :, j, :] = z[n2:]


def _fourier_out_body(pr_ref, pi_ref, cc_ref, sc_ref, w_ref, b_ref, res_ref, o_ref):
    pieces = []
    for g in range(N_FOURIER_GROUPS):
        sl = slice(g * GROUP_DIM, (g + 1) * GROUP_DIM)
        m = _dot(pr_ref[:, sl], cc_ref[...]) + _dot(pi_ref[:, sl], sc_ref[...])
        pieces.append(m.astype(BF16))
    o_ref[...] = res_ref[...] + (_dot(jnp.concatenate(pieces, axis=1), w_ref[...]) + b_ref[...])


def _dft_tables(L, n1, n2):
    two_pi = 2.0 * math.pi
    i2 = lax.broadcasted_iota(jnp.int32, (n2, n1, 1), 0)
    k1 = lax.broadcasted_iota(jnp.int32, (n2, n1, 1), 1)
    ang_a = ((k1 * i2) % L).astype(F32) * (two_pi / L)
    k1 = lax.broadcasted_iota(jnp.int32, (1, n1, n1), 1)
    i1 = lax.broadcasted_iota(jnp.int32, (1, n1, n1), 2)
    ang_b = ((k1 * i1) % n1).astype(F32) * (two_pi / n1)
    s1 = 1.0 / math.sqrt(n1)
    ca, sa, cb, sb = jnp.cos(ang_a) * s1, jnp.sin(ang_a) * s1, jnp.cos(ang_b), jnp.sin(ang_b)
    tab1 = jnp.concatenate([ca * cb - sa * sb, -(sa * cb + ca * sb)], axis=1).astype(BF16)
    a = lax.broadcasted_iota(jnp.int32, (n2, n2), 0)
    b = lax.broadcasted_iota(jnp.int32, (n2, n2), 1)
    ang2 = ((a * b) % n2).astype(F32) * (two_pi / n2)
    s2 = 1.0 / math.sqrt(n2)
    c2, sn2 = jnp.cos(ang2) * s2, jnp.sin(ang2) * s2
    tab2 = jnp.concatenate([jnp.concatenate([c2, sn2], axis=1),
                            jnp.concatenate([-sn2, c2], axis=1)], axis=0).astype(BF16)
    a = lax.broadcasted_iota(jnp.int32, (GROUP_DIM, GROUP_DIM), 0)
    b = lax.broadcasted_iota(jnp.int32, (GROUP_DIM, GROUP_DIM), 1)
    angc = ((a * b) % GROUP_DIM).astype(F32) * (two_pi / GROUP_DIM)
    sc = 1.0 / math.sqrt(GROUP_DIM)
    return tab1, tab2, (jnp.cos(angc) * sc).astype(BF16), (jnp.sin(angc) * sc).astype(BF16)


def _fourier_mixer(x, h, wout, bout, nb=8, tm=1024):
    L, D = x.shape
    n1 = n2 = math.isqrt(L)
    assert n1 * n2 == L
    tab1, tab2, cc, sc = _dft_tables(L, n1, n2)
    y = pl.pallas_call(
        _fourier1_body,
        grid=(n2 // nb,),
        in_specs=[pl.BlockSpec((n1, nb * D), lambda i: (0, i)),
                  pl.BlockSpec((nb, 2 * n1, n1), lambda i: (i, 0, 0))],
        out_specs=pl.BlockSpec((2, nb, n1, D), lambda i: (0, i, 0, 0)),
        out_shape=jax.ShapeDtypeStruct((2, n2, n1, D), BF16),
        compiler_params=_params(("parallel",), n1 * nb * D * 2, nb * 2 * n1 * n1 * 2, 2 * nb * n1 * D * 2),
        name="fourier_stage1",
    )(h.reshape(n1, n2 * D), tab1)
    nb2 = 16
    z = pl.pallas_call(
        _fourier2_body,
        grid=(n1 // nb2,),
        in_specs=[pl.BlockSpec((2 * n2, 2 * n2), lambda i: (0, 0)),
                  pl.BlockSpec((2 * n2, nb2 * D), lambda i: (0, i))],
        out_specs=pl.BlockSpec((2, n2, nb2, D), lambda i: (0, 0, i, 0)),
        out_shape=jax.ShapeDtypeStruct((2, n2, n1, D), BF16),
        compiler_params=_params(("parallel",), 2 * 2 * n2 * nb2 * D * 2),
        name="fourier_stage2",
    )(tab2, y.reshape(2 * n2, n1 * D))
    p = z.reshape(2, L, D)
    return pl.pallas_call(
        _fourier_out_body,
        grid=(L // tm,),
        in_specs=[pl.BlockSpec((None, tm, D), lambda i: (0, i, 0)),
                  pl.BlockSpec((None, tm, D), lambda i: (1, i, 0)),
                  pl.BlockSpec((GROUP_DIM, GROUP_DIM), lambda i: (0, 0)),
                  pl.BlockSpec((GROUP_DIM, GROUP_DIM), lambda i: (0, 0)),
                  pl.BlockSpec((D, D), lambda i: (0, 0)),
                  pl.BlockSpec((1, D), lambda i: (0, 0)),
                  pl.BlockSpec((tm, D), lambda i: (i, 0))],
        out_specs=pl.BlockSpec((tm, D), lambda i: (i, 0)),
        out_shape=jax.ShapeDtypeStruct((L, D), F32),
        compiler_params=_params(("parallel",), 2 * tm * D * 2, D * D * 2, 2 * tm * D * 4),
        name="fourier_out_proj",
    )(p, p, cc, sc, wout, bout, x)


def _kh_columns(a):
    lead = a.shape[:-1]
    q = a[..., :KEY_DIM].reshape(*lead, N_K_HEADS, HEAD_DIM)
    k = a[..., KEY_DIM:2 * KEY_DIM].reshape(*lead, N_K_HEADS, HEAD_DIM)
    v = a[..., 2 * KEY_DIM:].reshape(*lead, N_K_HEADS, 2 * HEAD_DIM)
    return jnp.concatenate([q, k, v], axis=-1).reshape(*lead, CONV_DIM)


def _gate_column_order():
    cols = []
    for h in range(N_K_HEADS):
        units = [d * N_V_HEADS + 2 * h + s for d in range(N_DIRS) for s in range(2)]
        cols += [N_DIRS * N_V_HEADS + un for un in units] + units
    return jnp.asarray(cols, jnp.int32)


def _unit_rows(p):
    per = p.reshape(N_DIRS, N_K_HEADS, 2).transpose(1, 0, 2).reshape(N_K_HEADS, 2 * N_DIRS)
    per = jnp.concatenate([per, jnp.zeros_like(per)], axis=1)
    return jnp.broadcast_to(per[:, :, None], (N_K_HEADS, N_GATE_ROWS, LANES_V7X)).astype(F32)


def _deltanet_layer(x, nw, w_in, conv_w, a_log, dt_bias, out_norm_w, w_out):
    L = x.shape[0]
    nc = L // CHUNK
    gate0 = CONV_DIM + VALUE_DIM
    wqkv = _kh_columns(w_in[:, :CONV_DIM]).astype(BF16)
    wz = w_in[:, CONV_DIM:gate0].astype(BF16)
    n_gate = 2 * N_DIRS * N_V_HEADS
    wg = jnp.pad(w_in[:, gate0:][:, _gate_column_order()].T, ((0, LANES_V7X - n_gate), (0, 0))).astype(BF16)
    cw = jnp.pad(_kh_columns(conv_w), ((0, SUBLANES_V7X - CONV_WIDTH), (0, 0)))

    y, z, gates = _in_proj(x, nw, wqkv, cw, wz, wg)
    w, u, qd, qk, kdt, gl = _dn_prep(y, gates, _unit_rows(a_log), _unit_rows(dt_bias))
    nu = N_DIRS * N_K_HEADS
    o = _dn_scan(w.reshape(nu, L, -1), u.reshape(nu, L, -1), qd.reshape(nu, L, -1), qk.reshape(nu, L, -1),
                 kdt.reshape(nu, nc * 2 * CHUNK, -1), gl.reshape(nu, nc, -1))
    return _dn_out(o, z, out_norm_w.reshape(1, -1), w_out.astype(BF16), x)


def kernel(x, mix_norm_w, ffn_norm_w, dn_w_in, dn_conv_w, dn_a_log, dn_dt_bias, dn_out_norm_w, dn_w_out,
           fn_w_out, fn_b_out, ffn_w_gate_up, ffn_w_down, final_norm_w):
    B, L, D = x.shape
    row = lambda v: v.reshape(1, -1)
    wgu, wdn = ffn_w_gate_up.astype(BF16), ffn_w_down
    outs = []
    for b in range(B):
        h = x[b]
        h = _deltanet_layer(h, row(mix_norm_w[0]), dn_w_in[0], dn_conv_w[0], dn_a_log[0], dn_dt_bias[0],
                            dn_out_norm_w[0], dn_w_out[0])
        a = _ffn_up(h, row(ffn_norm_w[0]), wgu, 0)
        h, hn = _ffn_down(a, wdn, 0, h, row(mix_norm_w[1]), final_norm=False, emit_norm=True)
        h = _fourier_mixer(h, hn, fn_w_out[0].astype(BF16), row(fn_b_out[0]))
        a = _ffn_up(h, row(ffn_norm_w[1]), wgu, 1)
        h = _ffn_down(a, wdn, 1, h, row(final_norm_w), final_norm=True)
        outs.append(h)
    return jnp.stack(outs, axis=0)
```
